```python
import math
import jax
import jax.numpy as jnp
from jax import lax
import numpy as np

D_MODEL = 2048
BATCH = 16
SEQ = 2048
DEPTH = 1

HEAD_DIM = 128
ATTN_GROUPS = ((128, 1), (512, 4), (2048, 16))
ATTN_HEADS_PER_GROUP = D_MODEL // 512
ATTN_HEADS = ATTN_HEADS_PER_GROUP * len(ATTN_GROUPS)
ATTN_WIDTH = ATTN_HEADS * HEAD_DIM
ATTN_OUT_WIDTH = ATTN_HEADS_PER_GROUP * HEAD_DIM
ATTN_BLOCK = 128
ROPE_THETA = 500000.0
ROPE_DIM = HEAD_DIM // 4
DN_HEADS = D_MODEL // 128
DN_KEY_DIM = 128
DN_VAL_DIM = 128
DN_QK_WIDTH = DN_HEADS * DN_KEY_DIM
DN_V_WIDTH = DN_HEADS * DN_VAL_DIM
DN_CONV_WIDTH = 4
DN_CHUNK = 64
PEER_HEADS = 8
PEER_KEYS = 128
PEER_EXPERTS = PEER_KEYS * PEER_KEYS
PEER_QUERY_DIM = 128
PEER_TOPK = 16
PEER_TOKEN_BLOCK = 128
NORM_EPS = 1e-6
IN_SPLIT_SIZES = (3 * ATTN_WIDTH, 2 * DN_QK_WIDTH + DN_V_WIDTH, DN_V_WIDTH, DN_HEADS, DN_HEADS, 2 * D_MODEL)
IN_WIDTH = sum(IN_SPLIT_SIZES)

kernel_name = 'hybrid_dilated_attn_deltanet_peer_block'


def rmsnorm(x, w):
    xf = x.astype(jnp.float32)
    y = xf * lax.rsqrt(jnp.mean(xf * xf, axis=-1, keepdims=True) + NORM_EPS)
    return (y * w.astype(jnp.float32)).astype(x.dtype)


def l2norm(t):
    return t * lax.rsqrt(jnp.sum(t * t, axis=-1, keepdims=True) + 1e-6)


def partial_rotary(t, positions):
    half = ROPE_DIM // 2
    inv_freq = ROPE_THETA ** (-jnp.arange(half, dtype=jnp.float32) * (2.0 / ROPE_DIM))
    ang = positions.astype(jnp.float32)[:, None] * inv_freq[None, :]
    cos = jnp.cos(ang)[None, :, None, :]
    sin = jnp.sin(ang)[None, :, None, :]
    tf = t.astype(jnp.float32)
    t1, t2 = tf[..., :half], tf[..., half:ROPE_DIM]
    out = jnp.concatenate([t1 * cos - t2 * sin, t2 * cos + t1 * sin, tf[..., ROPE_DIM:]], axis=-1)
    return out.astype(t.dtype)


def dilated_window_attention(q, k, v, window, dilation):
    B, S, H, hd = q.shape
    n_back = window // dilation
    L = S // dilation
    nb = -(-L // ATTN_BLOCK)
    Lp = nb * ATTN_BLOCK

    def to_blocks(t):
        t = t.reshape(B, L, dilation, H, hd).transpose(0, 2, 1, 3, 4)
        t = jnp.pad(t, ((0, 0), (0, 0), (0, Lp - L), (0, 0), (0, 0)))
        return t.reshape(B, dilation, nb, ATTN_BLOCK, H, hd)

    def band(t):
        prev = jnp.pad(t, ((0, 0), (0, 0), (1, 0), (0, 0), (0, 0), (0, 0)))[:, :, :-1]
        return jnp.concatenate([prev, t], axis=3)

    qb = to_blocks(q).astype(jnp.float32)
    kk = band(to_blocks(k)).astype(jnp.float32)
    vv = band(to_blocks(v)).astype(jnp.float32)
    s = jnp.einsum('brnqhd,brnkhd->brnhqk', qb, kk) * (hd ** -0.5)
    qi = jnp.arange(ATTN_BLOCK)[:, None]
    kj = jnp.arange(2 * ATTN_BLOCK)[None, :]
    dist = qi + ATTN_BLOCK - kj
    valid = (dist >= 0) & (dist <= n_back)
    not_before_start = (jnp.arange(nb)[:, None, None] > 0) | (kj[None] >= ATTN_BLOCK)
    mask = valid[None] & not_before_start
    s = jnp.where(mask[None, None, :, None], s, -jnp.inf)
    lse = jax.nn.logsumexp(s, axis=-1)
    p = jnp.exp(s - lse[..., None])
    o = jnp.einsum('brnhqk,brnkhd->brnqhd', p, vv)
    o = o.reshape(B, dilation, Lp, H, hd)[:, :, :L].transpose(0, 2, 1, 3, 4).reshape(B, S, H, hd)
    lse = lse.transpose(0, 1, 2, 4, 3).reshape(B, dilation, Lp, H)[:, :, :L]
    lse = lse.transpose(0, 2, 1, 3).reshape(B, S, H)
    return o, lse


def short_conv(x, w):
    C = x.shape[-1]
    return lax.conv_general_dilated(
        x, w.astype(x.dtype)[:, None, :], window_strides=(1,),
        padding=((DN_CONV_WIDTH - 1, 0),), dimension_numbers=('NWC', 'WIO', 'NWC'),
        feature_group_count=C)


def gated_delta_rule(q, k, v, g, beta):
    B, S, H, dk = q.shape
    dv = v.shape[-1]
    C = DN_CHUNK
    nc = S // C

    def chunks(t):
        return t.reshape(B, nc, C, H, -1).transpose(1, 0, 3, 2, 4)

    qc, kc, vc = chunks(q), chunks(k), chunks(v)
    gc = g.reshape(B, nc, C, H).transpose(1, 0, 3, 2)
    bc = beta.reshape(B, nc, C, H).transpose(1, 0, 3, 2)
    gam = jnp.cumsum(gc, axis=-1)
    i = jnp.arange(C)[:, None]
    j = jnp.arange(C)[None, :]
    diff = gam[..., :, None] - gam[..., None, :]
    decay = jnp.where(i >= j, jnp.exp(jnp.where(i >= j, diff, 0.0)), 0.0)
    kb = kc * bc[..., None]
    lmat = jnp.where(i > j, jnp.einsum('nbhid,nbhjd->nbhij', kb, kc) * decay, 0.0)
    rhs = jnp.concatenate([vc * bc[..., None], kb * jnp.exp(gam)[..., None]], axis=-1)
    sol = lax.linalg.triangular_solve(lmat, rhs, left_side=True, lower=True, unit_diagonal=True)
    u, w = sol[..., :dv], sol[..., dv:]
    qk = jnp.einsum('nbhid,nbhjd->nbhij', qc, kc) * decay
    q_dec = qc * jnp.exp(gam)[..., None]
    k_dec = kc * jnp.exp(gam[..., -1:] - gam)[..., None]
    chunk_decay = jnp.exp(gam[..., -1])

    def step(state, inp):
        u_c, w_c, qk_c, qd_c, kd_c, cd_c = inp
        v_new = u_c - jnp.einsum('bhck,bhkv->bhcv', w_c, state)
        o = jnp.einsum('bhck,bhkv->bhcv', qd_c, state) + jnp.einsum('bhij,bhjv->bhiv', qk_c, v_new)
        state = state * cd_c[..., None, None] + jnp.einsum('bhck,bhcv->bhkv', kd_c, v_new)
        return state, o

    state0 = jnp.zeros((B, H, dk, dv), jnp.float32)
    _, o = lax.scan(step, state0, (u, w, qk, q_dec, k_dec, chunk_decay))
    return o.transpose(1, 0, 3, 2, 4).reshape(B, S, H, dv)


def token_mixer(h, w_in, conv_w, a_log, dt_bias, dn_norm_w, w_attn_out, w_delta_out, w_mix_out):
    B, S, _ = h.shape
    proj = h @ w_in
    split_points = [int(p) for p in np.cumsum(IN_SPLIT_SIZES)[:-1]]
    attn_qkv, dn_qkv, dn_z, dn_b, dn_a, gates = jnp.split(proj, split_points, axis=-1)

    attn_qkv = attn_qkv.reshape(B, S, 3, ATTN_HEADS, HEAD_DIM)
    positions = jnp.arange(S)
    qa = partial_rotary(attn_qkv[:, :, 0], positions)
    ka = partial_rotary(attn_qkv[:, :, 1], positions)
    va = attn_qkv[:, :, 2]
    outs, lses = [], []
    for gi, (window, dilation) in enumerate(ATTN_GROUPS):
        hs = slice(gi * ATTN_HEADS_PER_GROUP, (gi + 1) * ATTN_HEADS_PER_GROUP)
        o, lse = dilated_window_attention(qa[:, :, hs], ka[:, :, hs], va[:, :, hs], window, dilation)
        outs.append(o)
        lses.append(lse)
    group_w = jax.nn.softmax(jnp.stack(lses, axis=0), axis=0)
    y_attn = jnp.sum(group_w[..., None] * jnp.stack(outs, axis=0), axis=0)
    y_attn = y_attn.reshape(B, S, ATTN_OUT_WIDTH).astype(h.dtype) @ w_attn_out

    dn_qkv = jax.nn.silu(short_conv(dn_qkv, conv_w))
    dq, dk, dvv = jnp.split(dn_qkv, [DN_QK_WIDTH, 2 * DN_QK_WIDTH], axis=-1)
    dq = l2norm(dq.reshape(B, S, DN_HEADS, DN_KEY_DIM).astype(jnp.float32)) * (DN_KEY_DIM ** -0.5)
    dk = l2norm(dk.reshape(B, S, DN_HEADS, DN_KEY_DIM).astype(jnp.float32))
    dvv = dvv.reshape(B, S, DN_HEADS, DN_VAL_DIM).astype(jnp.float32)
    beta = jax.nn.sigmoid(dn_b.astype(jnp.float32))
    g = -jnp.exp(a_log.astype(jnp.float32)) * jax.nn.softplus(dn_a.astype(jnp.float32) + dt_bias.astype(jnp.float32))
    o = gated_delta_rule(dq, dk, dvv, g, beta)
    o = rmsnorm(o, dn_norm_w) * jax.nn.silu(dn_z.reshape(B, S, DN_HEADS, DN_VAL_DIM).astype(jnp.float32))
    y_delta = o.reshape(B, S, DN_V_WIDTH).astype(h.dtype) @ w_delta_out

    gate_attn, gate_delta = jnp.split(jax.nn.sigmoid(gates), 2, axis=-1)
    merged = gate_attn * y_attn + gate_delta * y_delta
    return merged @ w_mix_out


def peer_ffn(h, w_query, sub_keys, expert_down, expert_up):
    B, S, D = h.shape
    T = B * S
    hb = h.reshape(T // PEER_TOKEN_BLOCK, PEER_TOKEN_BLOCK, D)

    def block(xb):
        tb = xb.shape[0]
        q = (xb @ w_query).reshape(tb, PEER_HEADS, 2, PEER_QUERY_DIM // 2).astype(jnp.float32)
        s = jnp.einsum('thpd,hpnd->thpn', q, sub_keys.astype(jnp.float32))
        half_s, half_i = lax.top_k(s, PEER_TOPK)
        cand_s = (half_s[:, :, 0, :, None] + half_s[:, :, 1, None, :]).reshape(tb, PEER_HEADS, PEER_TOPK * PEER_TOPK)
        cand_i = (half_i[:, :, 0, :, None] * PEER_KEYS + half_i[:, :, 1, None, :]).reshape(tb, PEER_HEADS, PEER_TOPK * PEER_TOPK)
        best_s, best_pos = lax.top_k(cand_s, PEER_TOPK)
        idx = jnp.take_along_axis(cand_i, best_pos, axis=-1)
        gw = jax.nn.softmax(best_s, axis=-1)
        u = jnp.take(expert_down, idx, axis=0)
        act = jax.nn.gelu(jnp.einsum('thkd,td->thk', u, xb).astype(jnp.float32), approximate=False)
        coef = (gw * act).astype(xb.dtype)
        vv = jnp.take(expert_up, idx, axis=0)
        return jnp.einsum('thk,thkd->td', coef, vv)

    out = lax.map(block, hb)
    return out.reshape(B, S, D)


def setup_inputs(seed: int = 0) -> dict:
    key = jax.random.key(seed)
    ks = jax.random.split(key, 20)
    f32 = jnp.float32

    def normal(k, shape, scale):
        return jax.random.normal(k, shape, f32) * scale

    def gain(k, shape):
        return 1.0 + normal(k, shape, 0.02)

    dt = jnp.exp(jax.random.uniform(ks[10], (DEPTH, DN_HEADS), f32, math.log(1e-3), math.log(1e-1)))
    return {
        'x': normal(ks[0], (BATCH, SEQ, D_MODEL), 1.0),
        'c': normal(ks[1], (BATCH, D_MODEL), 1.0),
        'w_ada': normal(ks[2], (DEPTH, D_MODEL, 6 * D_MODEL), 0.5 * D_MODEL ** -0.5),
        'b_ada': normal(ks[3], (DEPTH, 6 * D_MODEL), 0.02),
        'norm_pre_mix': gain(ks[4], (DEPTH, D_MODEL)),
        'norm_post_mix': gain(ks[5], (DEPTH, D_MODEL)),
        'norm_pre_ffn': gain(ks[6], (DEPTH, D_MODEL)),
        'norm_post_ffn': gain(ks[7], (DEPTH, D_MODEL)),
        'w_in': normal(ks[8], (DEPTH, D_MODEL, IN_WIDTH), D_MODEL ** -0.5),
        'conv_w': normal(ks[9], (DEPTH, DN_CONV_WIDTH, 2 * DN_QK_WIDTH + DN_V_WIDTH), DN_CONV_WIDTH ** -0.5),
        'a_log': jnp.log(jax.random.uniform(ks[11], (DEPTH, DN_HEADS), f32, 1.0, 16.0)),
        'dt_bias': dt + jnp.log(-jnp.expm1(-dt)),
        'dn_norm_w': gain(ks[12], (DEPTH, DN_VAL_DIM)),
        'w_attn_out': normal(ks[13], (DEPTH, ATTN_OUT_WIDTH, D_MODEL), ATTN_OUT_WIDTH ** -0.5),
        'w_delta_out': normal(ks[14], (DEPTH, DN_V_WIDTH, D_MODEL), DN_V_WIDTH ** -0.5),
        'w_mix_out': normal(ks[15], (DEPTH, D_MODEL, D_MODEL), D_MODEL ** -0.5),
        'peer_w_query': normal(ks[16], (DEPTH, D_MODEL, PEER_HEADS * PEER_QUERY_DIM), D_MODEL ** -0.5),
        'peer_sub_keys': normal(ks[17], (DEPTH, PEER_HEADS, 2, PEER_KEYS, PEER_QUERY_DIM // 2), (PEER_QUERY_DIM // 2) ** -0.5),
        'peer_down': normal(ks[18], (DEPTH, PEER_EXPERTS, D_MODEL), D_MODEL ** -0.5),
        'peer_up': normal(ks[19], (DEPTH, PEER_EXPERTS, D_MODEL), (PEER_HEADS * PEER_TOPK) ** -0.5),
    }


def reference(x, c, w_ada, b_ada, norm_pre_mix, norm_post_mix, norm_pre_ffn, norm_post_ffn,
              w_in, conv_w, a_log, dt_bias, dn_norm_w, w_attn_out, w_delta_out, w_mix_out,
              peer_w_query, peer_sub_keys, peer_down, peer_up):
    cond = jax.nn.silu(c)
    for layer in range(DEPTH):
        ada = cond @ w_ada[layer] + b_ada[layer]
        shift_m, scale_m, gate_m, shift_f, scale_f, gate_f = [t[:, None, :] for t in jnp.split(ada, 6, axis=-1)]
        h = rmsnorm(x, norm_pre_mix[layer]) * (1 + scale_m) + shift_m
        y = token_mixer(h, w_in[layer], conv_w[layer], a_log[layer], dt_bias[layer], dn_norm_w[layer],
                        w_attn_out[layer], w_delta_out[layer], w_mix_out[layer])
        x = x + gate_m * rmsnorm(y, norm_post_mix[layer])
        h = rmsnorm(x, norm_pre_ffn[layer]) * (1 + scale_f) + shift_f
        y = peer_ffn(h, peer_w_query[layer], peer_sub_keys[layer], peer_down[layer], peer_up[layer])
        x = x + gate_f * rmsnorm(y, norm_post_ffn[layer])
    return x
```

```python
import functools
import math

import jax
import jax.numpy as jnp
from jax import lax
from jax.experimental import pallas as pl
from jax.experimental.pallas import tpu as pltpu

F32 = jnp.float32
BF16 = jnp.bfloat16
HIGHEST = lax.Precision.HIGHEST
LANES = 128
SUBLANES = 8
VMEM_LIMIT = 56 * 1024 * 1024

NORM_EPS = 1e-6
HEAD_DIM = 128
ATTN_GROUPS = ((128, 1), (512, 4), (2048, 16))
ATTN_HPG = 4
ATTN_HEADS = ATTN_HPG * len(ATTN_GROUPS)
ATTN_WIDTH = ATTN_HEADS * HEAD_DIM
ATTN_BLOCK = 128
ROPE_THETA = 500000.0
ROPE_DIM = HEAD_DIM // 4
DN_HEADS = 16
DN_DIM = 128
DN_WIDTH = DN_HEADS * DN_DIM
DN_CONV = 4
DN_CHUNK = 64
DN_HB = 2
PEER_HEADS = 8
PEER_KEYS = 128
PEER_QDIM = 128
PEER_TOPK = 16
PEER_SLOTS = PEER_HEADS * PEER_TOPK
PEER_TB = 128
PEER_NBUF = 8
PEER_PITCH = 17

COL_GA = 0
COL_GD = COL_GA + 2048
COL_DQ = COL_GD + 2048
COL_DK = COL_DQ + DN_WIDTH
COL_DV = COL_DK + DN_WIDTH
COL_DZ = COL_DV + DN_WIDTH
COL_AQ = COL_DZ + DN_WIDTH
COL_AK = COL_AQ + ATTN_WIDTH
COL_AV = COL_AK + ATTN_WIDTH
COL_BA = COL_AV + ATTN_WIDTH
PROJ_WIDTH = 17 * 1024


def _silu(x):
    return x * jax.nn.sigmoid(x)


def _gelu_exact(x):
    return 0.5 * x * (1.0 + lax.erf(x * (2.0 ** -0.5)))


def _dot(a, b, precision=None):
    return jnp.dot(a, b, preferred_element_type=F32, precision=precision)


def _dot_nt(a, b, precision=None):
    return lax.dot_general(a, b, (((1,), (1,)), ((), ())), preferred_element_type=F32, precision=precision)


def _dot_tn(a, b, precision=None):
    return lax.dot_general(a, b, (((0,), (0,)), ((), ())), preferred_element_type=F32, precision=precision)


def _params(*sem):
    return pltpu.CompilerParams(dimension_semantics=sem, vmem_limit_bytes=VMEM_LIMIT)


def _ada_kernel(c_ref, w_ref, b_ref, o_ref):
    o_ref[...] = _dot(_silu(c_ref[...]), w_ref[...], HIGHEST) + b_ref[...]


def _ada(c, w, b):
    bsz, d = c.shape
    n = w.shape[1]
    tn = 1536
    return pl.pallas_call(
        _ada_kernel,
        grid=(n // tn,),
        in_specs=[pl.BlockSpec((bsz, d), lambda j: (0, 0)),
                  pl.BlockSpec((d, tn), lambda j: (0, j)),
                  pl.BlockSpec((1, tn), lambda j: (0, j))],
        out_specs=pl.BlockSpec((bsz, tn), lambda j: (0, j)),
        out_shape=jax.ShapeDtypeStruct((bsz, n), F32),
        compiler_params=_params("parallel"),
        name="ada",
    )(c, w, b.reshape(1, n))


def _rope_kernel(cos_ref, sin_ref):
    rows = cos_ref.shape[0]
    pos = (lax.broadcasted_iota(jnp.int32, (rows, LANES), 0) + pl.program_id(0) * rows).astype(F32)
    lane = lax.broadcasted_iota(jnp.int32, (rows, LANES), 1)
    half = ROPE_DIM // 2
    inv_freq = jnp.exp((lane % half).astype(F32) * (-(2.0 / ROPE_DIM) * math.log(ROPE_THETA)))
    ang = pos * inv_freq
    cos_ref[...] = jnp.where(lane < ROPE_DIM, jnp.cos(ang), 1.0)
    sin = jnp.sin(ang)
    sin_ref[...] = jnp.where(lane < half, -sin, jnp.where(lane < ROPE_DIM, sin, 0.0))


def _rope_tables(seq):
    rows = 256
    return pl.pallas_call(
        _rope_kernel,
        grid=(seq // rows,),
        out_specs=[pl.BlockSpec((rows, LANES), lambda i: (i, 0))] * 2,
        out_shape=[jax.ShapeDtypeStruct((seq, LANES), F32)] * 2,
        compiler_params=_params("parallel"),
        name="rope",
    )()


def _inproj_kernel(x_ref, ada_ref, nw_ref, w_ref, o_ref, h_ref):
    @pl.when(pl.program_id(1) == 0)
    def _():
        x = x_ref[...]
        y = x * lax.rsqrt(jnp.mean(x * x, axis=-1, keepdims=True) + NORM_EPS) * nw_ref[...]
        h_ref[...] = (y * (1.0 + ada_ref[1:2, :]) + ada_ref[0:1, :]).astype(BF16)

    o_ref[...] = _dot(h_ref[...], w_ref[...])


def _inproj(x2, ada3, norm_w, w_bf16, seq):
    t, d = x2.shape
    n = w_bf16.shape[1]
    tm, tn = 1024, 1024
    return pl.pallas_call(
        _inproj_kernel,
        grid=(t // tm, n // tn),
        in_specs=[pl.BlockSpec((tm, d), lambda i, j: (i, 0)),
                  pl.BlockSpec((None, 6, d), lambda i, j: (i * tm // seq, 0, 0)),
                  pl.BlockSpec((1, d), lambda i, j: (0, 0)),
                  pl.BlockSpec((d, tn), lambda i, j: (0, j))],
        out_specs=pl.BlockSpec((tm, tn), lambda i, j: (i, j)),
        out_shape=jax.ShapeDtypeStruct((t, n), F32),
        scratch_shapes=[pltpu.VMEM((tm, d), BF16)],
        compiler_params=_params("parallel", "arbitrary"),
        name="in_proj",
    )(x2, ada3, norm_w, w_bf16)


def _attn_kernel(q1, k1, v1, q2, k2, v2, q3, k3, v3, cos_ref, sin_ref, o_ref, qs, ks, vs, acc_s, m_s, l_s):
    seq = q1.shape[0]
    nblk = seq // ATTN_BLOCK
    lane = lax.broadcasted_iota(jnp.int32, (seq, LANES), 1)
    half = ROPE_DIM // 2

    def rotary(t):
        partner = jnp.where(lane < half, pltpu.roll(t, LANES - half, 1), pltpu.roll(t, half, 1))
        return t * cos_ref[...] + partner * sin_ref[...]

    qi = lax.broadcasted_iota(jnp.int32, (ATTN_BLOCK, 2 * ATTN_BLOCK), 0)
    kj = lax.broadcasted_iota(jnp.int32, (ATTN_BLOCK, 2 * ATTN_BLOCK), 1)
    dist = qi + ATTN_BLOCK - kj
    scale = HEAD_DIM ** -0.5

    for g, (refs, (window, dil)) in enumerate(zip(((q1, k1, v1), (q2, k2, v2), (q3, k3, v3)), ATTN_GROUPS)):
        q_ref, k_ref, v_ref = refs
        n_back = window // dil
        pad = ATTN_BLOCK * dil
        in_window = (dist >= 0) & (dist <= n_back)
        qs[...] = rotary(q_ref[...])
        ks[0:pad, :] = jnp.zeros((pad, LANES), F32)
        vs[0:pad, :] = jnp.zeros((pad, LANES), F32)
        ks[pad:pad + seq, :] = rotary(k_ref[...])
        vs[pad:pad + seq, :] = v_ref[...]

        def block(idx, carry, g=g, dil=dil, pad=pad, in_window=in_window):
            res = lax.rem(idx, dil)
            nb = idx // dil
            start = res + nb * pad
            if dil == 1:
                rows_q = pl.ds(start, ATTN_BLOCK)
                rows_k = pl.ds(start, 2 * ATTN_BLOCK)
            else:
                rows_q = pl.ds(start, ATTN_BLOCK, stride=dil)
                rows_k = pl.ds(start, 2 * ATTN_BLOCK, stride=dil)
            s = _dot_nt(qs[rows_q, :], ks[rows_k, :]) * scale
            valid = in_window & ((nb > 0) | (kj >= ATTN_BLOCK))
            s = jnp.where(valid, s, -jnp.inf)
            m = jnp.max(s, axis=-1, keepdims=True)
            p = jnp.exp(s - m)
            acc_s[g, rows_q, :] = _dot(p, vs[rows_k, :])
            m_s[g, rows_q, :] = jnp.broadcast_to(m, (ATTN_BLOCK, LANES))
            l_s[g, rows_q, :] = jnp.broadcast_to(jnp.sum(p, axis=-1, keepdims=True), (ATTN_BLOCK, LANES))
            return carry

        lax.fori_loop(0, nblk, block, 0)

    def merge(i, carry):
        rows = pl.ds(pl.multiple_of(i * ATTN_BLOCK, ATTN_BLOCK), ATTN_BLOCK)
        ms = [m_s[g, rows, :] for g in range(3)]
        mx = jnp.maximum(jnp.maximum(ms[0], ms[1]), ms[2])
        ws = [jnp.exp(m - mx) for m in ms]
        num = ws[0] * acc_s[0, rows, :] + ws[1] * acc_s[1, rows, :] + ws[2] * acc_s[2, rows, :]
        den = ws[0] * l_s[0, rows, :] + ws[1] * l_s[1, rows, :] + ws[2] * l_s[2, rows, :]
        o_ref[rows, :] = (num / den).astype(o_ref.dtype)
        return carry

    lax.fori_loop(0, nblk, merge, 0)


def _attention(proj, cos_t, sin_t, bsz, seq):
    for window, dil in ATTN_GROUPS:
        assert window // dil == ATTN_BLOCK and seq % (ATTN_BLOCK * dil) == 0
    pad_max = ATTN_BLOCK * max(d for _, d in ATTN_GROUPS)

    def head_spec(col0, g):
        blk0 = col0 // HEAD_DIM + g * ATTN_HPG
        return pl.BlockSpec((seq, HEAD_DIM), lambda b, h: (b, blk0 + h))

    in_specs = []
    for g in range(len(ATTN_GROUPS)):
        in_specs += [head_spec(COL_AQ, g), head_spec(COL_AK, g), head_spec(COL_AV, g)]
    in_specs += [pl.BlockSpec((seq, LANES), lambda b, h: (0, 0))] * 2
    return pl.pallas_call(
        _attn_kernel,
        grid=(bsz, ATTN_HPG),
        in_specs=in_specs,
        out_specs=pl.BlockSpec((seq, HEAD_DIM), lambda b, h: (b, h)),
        out_shape=jax.ShapeDtypeStruct((bsz * seq, ATTN_HPG * HEAD_DIM), BF16),
        scratch_shapes=[pltpu.VMEM((seq, LANES), F32),
                        pltpu.VMEM((seq + pad_max, LANES), F32),
                        pltpu.VMEM((seq + pad_max, LANES), F32),
                        pltpu.VMEM((3, seq, LANES), F32),
                        pltpu.VMEM((3, seq, LANES), F32),
                        pltpu.VMEM((3, seq, LANES), F32)],
        compiler_params=_params("parallel", "parallel"),
        name="attention",
    )(*([proj] * 9), cos_t, sin_t)


def _deltanet_kernel(q_ref, k_ref, v_ref, z_ref, ba_ref, cwq_ref, cwk_ref, cwv_ref, alog_ref, dtb_ref, nw_ref,
                     o_ref, qp, kp, vp, u_s, w_s, qd_s, kd_s, qk_s, cd_s):
    seq = q_ref.shape[0]
    nchunk = seq // DN_CHUNK
    c64 = DN_CHUNK
    ri = lax.broadcasted_iota(jnp.int32, (c64, c64), 0)
    ci = lax.broadcasted_iota(jnp.int32, (c64, c64), 1)
    lower_incl = ri >= ci
    lower_strict = ri > ci
    ones_lower = lower_incl.astype(F32)
    eye = (ri == ci).astype(F32)
    lane = lax.broadcasted_iota(jnp.int32, (c64, LANES), 1)
    lane16 = lax.broadcasted_iota(jnp.int32, (1, DN_HEADS), 1)

    for hh in range(DN_HB):
        cols = slice(hh * DN_DIM, (hh + 1) * DN_DIM)
        for src, dst in ((q_ref, qp), (k_ref, kp), (v_ref, vp)):
            dst[hh, 0:SUBLANES, :] = jnp.zeros((SUBLANES, DN_DIM), F32)
            dst[hh, SUBLANES:SUBLANES + seq, :] = src[:, cols]

    def conv_silu(pad_ref, cw_ref, hh, r0):
        cols = slice(hh * DN_DIM, (hh + 1) * DN_DIM)
        win = pad_ref[hh, pl.ds(r0, c64 + SUBLANES), :]
        y = win[SUBLANES:, :] * cw_ref[DN_CONV - 1:DN_CONV, cols]
        for j in range(1, DN_CONV):
            y = y + pltpu.roll(win, j, 0)[SUBLANES:, :] * cw_ref[DN_CONV - 1 - j:DN_CONV - j, cols]
        return _silu(y)

    def local(c, carry):
        r0 = pl.multiple_of(c * c64, c64)
        for hh in range(DN_HB):
            head = pl.program_id(1) * DN_HB + hh
            qx = conv_silu(qp, cwq_ref, hh, r0)
            kx = conv_silu(kp, cwk_ref, hh, r0)
            vx = conv_silu(vp, cwv_ref, hh, r0)
            qn = qx * lax.rsqrt(jnp.sum(qx * qx, axis=-1, keepdims=True) + 1e-6) * (DN_DIM ** -0.5)
            kn = kx * lax.rsqrt(jnp.sum(kx * kx, axis=-1, keepdims=True) + 1e-6)
            ba = ba_ref[pl.ds(r0, c64), :]
            b_col = jnp.sum(jnp.where(lane == head, ba, 0.0), axis=-1, keepdims=True)
            a_col = jnp.sum(jnp.where(lane == head + DN_HEADS, ba, 0.0), axis=-1, keepdims=True)
            a_log = jnp.sum(jnp.where(lane16 == head, alog_ref[...], 0.0), axis=-1, keepdims=True)
            dt_b = jnp.sum(jnp.where(lane16 == head, dtb_ref[...], 0.0), axis=-1, keepdims=True)
            beta = jax.nn.sigmoid(b_col)
            ax = a_col + dt_b
            softplus = jnp.maximum(ax, 0.0) + jnp.log1p(jnp.exp(-jnp.abs(ax)))
            g = -jnp.exp(a_log) * softplus
            gm = jnp.where(lower_strict, jnp.broadcast_to(g, (c64, c64)), 0.0)
            diff = _dot(ones_lower, gm, HIGHEST)
            gam = diff[:, 0:1] + g[0:1, :]
            gam_last = gam[c64 - 1:c64, :]
            decay = jnp.where(lower_incl, jnp.exp(jnp.where(lower_incl, diff, 0.0)), 0.0)
            kb = kn * beta
            nmat = -jnp.where(lower_strict, _dot_nt(kb, kn) * decay, 0.0)
            inv = eye + nmat
            pw = nmat
            for _ in range(5):
                pw = _dot(pw, pw, HIGHEST)
                inv = inv + _dot(pw, inv, HIGHEST)
            eg = jnp.exp(gam)
            sol = _dot(inv, jnp.concatenate([vx * beta, kb * eg], axis=-1), HIGHEST)
            rows = pl.ds(r0, c64)
            u_s[hh, rows, :] = sol[:, :DN_DIM]
            w_s[hh, rows, :] = sol[:, DN_DIM:]
            qk_s[hh, rows, :] = _dot_nt(qn, kn) * decay
            qd_s[hh, rows, :] = qn * eg
            kd_s[hh, rows, :] = kn * jnp.exp(gam_last - gam)
            cd_s[hh, pl.ds(pl.multiple_of(c * SUBLANES, SUBLANES), SUBLANES), :] = jnp.broadcast_to(
                jnp.exp(gam_last), (SUBLANES, LANES))
        return carry

    lax.fori_loop(0, nchunk, local, 0)

    def scan(c, states):
        r0 = pl.multiple_of(c * c64, c64)
        rows = pl.ds(r0, c64)
        new_states = []
        for hh in range(DN_HB):
            cols = slice(hh * DN_DIM, (hh + 1) * DN_DIM)
            st = states[hh]
            cd = cd_s[hh, pl.ds(pl.multiple_of(c * SUBLANES, SUBLANES), SUBLANES), :][0:1, :]
            v_new = u_s[hh, rows, :] - _dot(w_s[hh, rows, :], st)
            o = _dot(qd_s[hh, rows, :], st) + _dot(qk_s[hh, rows, :], v_new)
            new_states.append(st * cd + _dot_tn(kd_s[hh, rows, :], v_new))
            on = o * lax.rsqrt(jnp.mean(o * o, axis=-1, keepdims=True) + NORM_EPS) * nw_ref[...]
            o_ref[rows, cols] = (on * _silu(z_ref[rows, cols])).astype(o_ref.dtype)
        return tuple(new_states)

    lax.fori_loop(0, nchunk, scan, tuple(jnp.zeros((DN_DIM, DN_DIM), F32) for _ in range(DN_HB)))


def _deltanet(proj, conv_w, a_log, dt_bias, norm_w, bsz, seq):
    wb = DN_DIM * DN_HB
    hblocks = DN_HEADS // DN_HB

    def act_spec(col0):
        blk0 = col0 // wb
        return pl.BlockSpec((seq, wb), lambda b, h: (b, blk0 + h))

    def cw_spec(part):
        blk0 = part * DN_WIDTH // wb
        return pl.BlockSpec((DN_CONV, wb), lambda b, h: (0, blk0 + h))

    return pl.pallas_call(
        _deltanet_kernel,
        grid=(bsz, hblocks),
        in_specs=[act_spec(COL_DQ), act_spec(COL_DK), act_spec(COL_DV), act_spec(COL_DZ),
                  pl.BlockSpec((seq, LANES), lambda b, h: (b, COL_BA // LANES)),
                  cw_spec(0), cw_spec(1), cw_spec(2),
                  pl.BlockSpec((1, DN_HEADS), lambda b, h: (0, 0)),
                  pl.BlockSpec((1, DN_HEADS), lambda b, h: (0, 0)),
                  pl.BlockSpec((1, DN_DIM), lambda b, h: (0, 0))],
        out_specs=pl.BlockSpec((seq, wb), lambda b, h: (b, h)),
        out_shape=jax.ShapeDtypeStruct((bsz * seq, DN_WIDTH), BF16),
        scratch_shapes=[pltpu.VMEM((DN_HB, seq + SUBLANES, DN_DIM), F32)] * 3
        + [pltpu.VMEM((DN_HB, seq, DN_DIM), F32)] * 4
        + [pltpu.VMEM((DN_HB, seq, DN_CHUNK), F32),
           pltpu.VMEM((DN_HB, seq // DN_CHUNK * SUBLANES, LANES), F32)],
        compiler_params=_params("parallel", "parallel"),
        name="deltanet",
    )(proj, proj, proj, proj, proj, conv_w, conv_w, conv_w,
      a_log.reshape(1, DN_HEADS), dt_bias.reshape(1, DN_HEADS), norm_w.reshape(1, DN_DIM))


def _mixout_kernel(ya_ref, og_ref, ga_ref, gd_ref, x_ref, ada_ref, wa_ref, wd_ref, wm_ref, npost_ref, npre_ref,
                   xmid_ref, h2_ref):
    y_attn = _dot(ya_ref[...], wa_ref[...])
    y_delta = _dot(og_ref[...], wd_ref[...])
    merged = jax.nn.sigmoid(ga_ref[...]) * y_attn + jax.nn.sigmoid(gd_ref[...]) * y_delta
    y = _dot(merged.astype(BF16), wm_ref[...])
    yn = y * lax.rsqrt(jnp.mean(y * y, axis=-1, keepdims=True) + NORM_EPS) * npost_ref[...]
    x_mid = x_ref[...] + ada_ref[2:3, :] * yn
    xmid_ref[...] = x_mid
    hn = x_mid * lax.rsqrt(jnp.mean(x_mid * x_mid, axis=-1, keepdims=True) + NORM_EPS) * npre_ref[...]
    h2_ref[...] = hn * (1.0 + ada_ref[4:5, :]) + ada_ref[3:4, :]


def _mixout(y_attn, o_gated, proj, x2, ada3, w_attn, w_delta, w_mix, norm_post, norm_pre_ffn, seq):
    t, d = x2.shape
    tm = 256
    const = dict(pipeline_mode=pl.Buffered(1))
    return pl.pallas_call(
        _mixout_kernel,
        grid=(t // tm,),
        in_specs=[pl.BlockSpec((tm, y_attn.shape[1]), lambda i: (i, 0)),
                  pl.BlockSpec((tm, d), lambda i: (i, 0)),
                  pl.BlockSpec((tm, d), lambda i: (i, COL_GA // d)),
                  pl.BlockSpec((tm, d), lambda i: (i, COL_GD // d)),
                  pl.BlockSpec((tm, d), lambda i: (i, 0)),
                  pl.BlockSpec((None, 6, d), lambda i: (i * tm // seq, 0, 0)),
                  pl.BlockSpec(w_attn.shape, lambda i: (0, 0), **const),
                  pl.BlockSpec(w_delta.shape, lambda i: (0, 0), **const),
                  pl.BlockSpec(w_mix.shape, lambda i: (0, 0), **const),
                  pl.BlockSpec((1, d), lambda i: (0, 0)),
                  pl.BlockSpec((1, d), lambda i: (0, 0))],
        out_specs=[pl.BlockSpec((tm, d), lambda i: (i, 0))] * 2,
        out_shape=[jax.ShapeDtypeStruct((t, d), F32)] * 2,
        compiler_params=_params("parallel"),
        name="mix_out",
    )(y_attn, o_gated, proj, proj, x2, ada3, w_attn, w_delta, w_mix, norm_post, norm_pre_ffn)


def _topk_rows(s, k):
    n = s.shape[0]
    rid = lax.broadcasted_iota(jnp.int32, s.shape, 0)
    vals, ids = [], []
    for _ in range(k):
        m = jnp.max(s, axis=0, keepdims=True)
        sel = jnp.min(jnp.where(s == m, rid, n), axis=0, keepdims=True)
        vals.append(m)
        ids.append(sel)
        s = jnp.where(rid == sel, -jnp.inf, s)
    return jnp.concatenate(vals, axis=0), jnp.concatenate(ids, axis=0)


def _peer_route_kernel(h_ref, wq_ref, keys_ref, idx_ref, gw_ref, q_s, idx_s, gw_s):
    tm = h_ref.shape[0]
    head = pl.program_id(1)
    half = PEER_QDIM // 2

    @pl.when(head == 0)
    def _():
        q = _dot(h_ref[...].astype(BF16), wq_ref[...])
        for hh in range(PEER_HEADS):
            q_s[hh] = q[:, hh * PEER_QDIM:(hh + 1) * PEER_QDIM]

    qh = q_s[head]
    tops = []
    for p in range(2):
        s = _dot_nt(keys_ref[0, p], qh[:, p * half:(p + 1) * half], HIGHEST)
        tops.append(_topk_rows(s, PEER_TOPK))
    (s0, i0), (s1, i1) = tops
    cand = jnp.concatenate([s0[a:a + 1, :] + s1 for a in range(PEER_TOPK)], axis=0)
    best_s, best_pos = _topk_rows(cand, PEER_TOPK)
    a_sel = lax.shift_right_logical(best_pos, PEER_TOPK.bit_length() - 1)
    b_sel = best_pos & (PEER_TOPK - 1)
    e0 = jnp.zeros_like(best_pos)
    e1 = jnp.zeros_like(best_pos)
    for a in range(PEER_TOPK):
        e0 = e0 + jnp.where(a_sel == a, i0[a:a + 1, :], 0)
        e1 = e1 + jnp.where(b_sel == a, i1[a:a + 1, :], 0)
    ex = jnp.exp(best_s - jnp.max(best_s, axis=0, keepdims=True))
    rows = pl.ds(pl.multiple_of(head * PEER_TOPK, PEER_TOPK), PEER_TOPK)
    idx_s[rows, :] = e0 * PEER_KEYS + e1
    gw_s[rows, :] = ex / jnp.sum(ex, axis=0, keepdims=True)

    @pl.when(head == PEER_HEADS - 1)
    def _():
        for bb in range(tm // PEER_TB):
            tok = slice(bb * PEER_TB, (bb + 1) * PEER_TB)
            gw_ref[bb] = gw_s[:, tok]
            idx_ref[bb] = idx_s[:, tok].T


def _peer_route(h2, w_query_bf16, keys):
    t, d = h2.shape
    tm = 256
    nb = tm // PEER_TB
    return pl.pallas_call(
        _peer_route_kernel,
        grid=(t // tm, PEER_HEADS),
        in_specs=[pl.BlockSpec((tm, d), lambda i, h: (i, 0)),
                  pl.BlockSpec(w_query_bf16.shape, lambda i, h: (0, 0)),
                  pl.BlockSpec((1, 2, PEER_KEYS, PEER_QDIM // 2), lambda i, h: (h, 0, 0, 0))],
        out_specs=[pl.BlockSpec((nb, PEER_TB, PEER_SLOTS), lambda i, h: (i, 0, 0)),
                   pl.BlockSpec((nb, PEER_SLOTS, PEER_TB), lambda i, h: (i, 0, 0))],
        out_shape=[jax.ShapeDtypeStruct((t // PEER_TB, PEER_TB, PEER_SLOTS), jnp.int32),
                   jax.ShapeDtypeStruct((t // PEER_TB, PEER_SLOTS, PEER_TB), F32)],
        scratch_shapes=[pltpu.VMEM((PEER_HEADS, tm, PEER_QDIM), F32),
                        pltpu.VMEM((PEER_SLOTS, tm), jnp.int32),
                        pltpu.VMEM((PEER_SLOTS, tm), F32)],
        compiler_params=_params("parallel", "arbitrary"),
        name="peer_route",
    )(h2, w_query_bf16, keys)


def _peer_gather_kernel(idx_ref, gw_ref, h_ref, xmid_ref, ada_ref, nw_ref, down_hbm, up_hbm,
                        out_ref, buf, sem, hmat, ybuf):
    tb, d = h_ref.shape
    nsub = d // LANES
    slot_rows = PEER_SLOTS * PEER_PITCH
    nbuf = PEER_NBUF
    lane = lax.broadcasted_iota(jnp.int32, (PEER_SLOTS, tb), 1)

    def row_copy(tbl, row, slot, k):
        return pltpu.make_async_copy(tbl.at[row], buf.at[pl.ds(slot * slot_rows + k * PEER_PITCH, nsub)],
                                     sem.at[slot])

    def issue(tbl, t, slot):
        for k in range(PEER_SLOTS):
            row_copy(tbl, idx_ref[0, t, k], slot, k).start(priority=k % 2)

    def wait(tbl, slot):
        rows = pl.ds(slot * slot_rows, PEER_SLOTS * nsub)
        pltpu.make_async_copy(buf.at[rows], buf.at[rows], sem.at[slot]).wait()

    def tile(slot, s):
        return buf[pl.ds(slot * slot_rows + s, PEER_SLOTS, stride=PEER_PITCH), :]

    def run_phase(tbl, body):
        for j in range(nbuf):
            issue(tbl, j, j)

        def group(g, carry):
            for j in range(nbuf):
                t = g * nbuf + j
                wait(tbl, j)
                body(t, j)
                issue(tbl, t + nbuf, j)
            return carry

        lax.fori_loop(0, tb // nbuf - 1, group, 0)
        for j in range(nbuf):
            wait(tbl, j)
            body(tb - nbuf + j, j)

    def body_down(t, slot):
        xrow = h_ref[pl.ds(t, 1), :]
        acc = tile(slot, 0) * xrow[:, 0:LANES]
        for s in range(1, nsub):
            acc = acc + tile(slot, s) * xrow[:, s * LANES:(s + 1) * LANES]
        hcol = jnp.sum(acc, axis=-1, keepdims=True)
        hmat[...] = jnp.where(lane == t, hcol, hmat[...])

    hmat[...] = jnp.zeros_like(hmat)
    run_phase(down_hbm, body_down)
    hmat[...] = gw_ref[0] * _gelu_exact(hmat[...])

    def body_up(t, slot):
        ccol = jnp.sum(jnp.where(lane == t, hmat[...], 0.0), axis=-1, keepdims=True)
        parts = [jnp.sum(tile(slot, s) * ccol, axis=0, keepdims=True) for s in range(nsub)]
        ybuf[pl.ds(t, 1), :] = jnp.concatenate(parts, axis=-1)

    run_phase(up_hbm, body_up)
    y = ybuf[...]
    yn = y * lax.rsqrt(jnp.mean(y * y, axis=-1, keepdims=True) + NORM_EPS) * nw_ref[...]
    out_ref[...] = xmid_ref[...] + ada_ref[5:6, :] * yn


def _peer_gather(idx_t, gw_t, h2, x_mid, ada3, norm_w, down3, up3, seq):
    t, d = h2.shape
    blocks_per_batch = seq // PEER_TB
    assert PEER_TB % PEER_NBUF == 0 and d // LANES < PEER_PITCH
    return pl.pallas_call(
        _peer_gather_kernel,
        grid=(t // PEER_TB,),
        in_specs=[pl.BlockSpec((1, PEER_TB, PEER_SLOTS), lambda i: (i, 0, 0), memory_space=pltpu.SMEM),
                  pl.BlockSpec((1, PEER_SLOTS, PEER_TB), lambda i: (i, 0, 0)),
                  pl.BlockSpec((PEER_TB, d), lambda i: (i, 0)),
                  pl.BlockSpec((PEER_TB, d), lambda i: (i, 0)),
                  pl.BlockSpec((None, 6, d), lambda i: (i // blocks_per_batch, 0, 0)),
                  pl.BlockSpec((1, d), lambda i: (0, 0)),
                  pl.BlockSpec(memory_space=pl.ANY),
                  pl.BlockSpec(memory_space=pl.ANY)],
        out_specs=pl.BlockSpec((PEER_TB, d), lambda i: (i, 0)),
        out_shape=jax.ShapeDtypeStruct((t, d), F32),
        scratch_shapes=[pltpu.VMEM((PEER_NBUF * PEER_SLOTS * PEER_PITCH, LANES), F32),
                        pltpu.SemaphoreType.DMA((PEER_NBUF,)),
                        pltpu.VMEM((PEER_SLOTS, PEER_TB), F32),
                        pltpu.VMEM((PEER_TB, d), F32)],
        compiler_params=_params("arbitrary"),
        name="peer_gather",
    )(idx_t, gw_t, h2, x_mid, ada3, norm_w, down3, up3)


def _permute_w_in(w_in):
    o_attn, o_dn, o_z = 0, 3 * ATTN_WIDTH, 3 * ATTN_WIDTH + 3 * DN_WIDTH
    o_b = o_z + DN_WIDTH
    o_gates = o_b + 2 * DN_HEADS
    parts = [w_in[:, o_gates:o_gates + 4096], w_in[:, o_dn:o_dn + 3 * DN_WIDTH], w_in[:, o_z:o_z + DN_WIDTH],
             w_in[:, o_attn:o_attn + 3 * ATTN_WIDTH], w_in[:, o_b:o_b + 2 * DN_HEADS]]
    w = jnp.concatenate(parts, axis=1)
    return jnp.pad(w, ((0, 0), (0, PROJ_WIDTH - w.shape[1]))).astype(BF16)


def _layer(x2, c, bsz, seq, w_ada, b_ada, norm_pre_mix, norm_post_mix, norm_pre_ffn, norm_post_ffn, w_in, conv_w,
           a_log, dt_bias, dn_norm_w, w_attn_out, w_delta_out, w_mix_out, peer_w_query, peer_sub_keys, peer_down,
           peer_up, cos_t, sin_t):
    d = x2.shape[1]
    row = lambda v: v.reshape(1, -1)
    ada3 = _ada(c, w_ada, b_ada).reshape(bsz, 6, d)
    proj = _inproj(x2, ada3, row(norm_pre_mix), _permute_w_in(w_in), seq)
    y_attn = _attention(proj, cos_t, sin_t, bsz, seq)
    o_gated = _deltanet(proj, conv_w, a_log, dt_bias, dn_norm_w, bsz, seq)
    x_mid, h2 = _mixout(y_attn, o_gated, proj, x2, ada3, w_attn_out.astype(BF16), w_delta_out.astype(BF16),
                        w_mix_out.astype(BF16), row(norm_post_mix), row(norm_pre_ffn), seq)
    idx_t, gw_t = _peer_route(h2, peer_w_query.astype(BF16), peer_sub_keys)
    nsub = d // LANES
    down3 = peer_down.reshape(peer_down.shape[0], nsub, LANES)
    up3 = peer_up.reshape(peer_up.shape[0], nsub, LANES)
    return _peer_gather(idx_t, gw_t, h2, x_mid, ada3, row(norm_post_ffn), down3, up3, seq)


def kernel(x, c, w_ada, b_ada, norm_pre_mix, norm_post_mix, norm_pre_ffn, norm_post_ffn, w_in, conv_w, a_log, dt_bias, dn_norm_w, w_attn_out, w_delta_out, w_mix_out, peer_w_query, peer_sub_keys, peer_down, peer_up):
    bsz, seq, d = x.shape
    x2 = x.reshape(bsz * seq, d)
    cos_t, sin_t = _rope_tables(seq)
    for layer in range(w_ada.shape[0]):
        x2 = _layer(x2, c, bsz, seq, w_ada[layer], b_ada[layer], norm_pre_mix[layer], norm_post_mix[layer],
                    norm_pre_ffn[layer], norm_post_ffn[layer], w_in[layer], conv_w[layer], a_log[layer],
                    dt_bias[layer], dn_norm_w[layer], w_attn_out[layer], w_delta_out[layer], w_mix_out[layer],
                    peer_w_query[layer], peer_sub_keys[layer], peer_down[layer], peer_up[layer], cos_t, sin_t)
    return x2.reshape(bsz, seq, d)
```

```python
import functools
import math

import jax
import jax.numpy as jnp
from jax import lax
from jax.experimental import pallas as pl
from jax.experimental.pallas import tpu as pltpu

F32 = jnp.float32
BF16 = jnp.bfloat16
HIGHEST = lax.Precision.HIGHEST
LANES = 128
SUBLANES = 8
VMEM_LIMIT = 56 * 1024 * 1024

NORM_EPS = 1e-6
HEAD_DIM = 128
ATTN_GROUPS = ((128, 1), (512, 4), (2048, 16))
ATTN_HPG = 4
ATTN_HEADS = ATTN_HPG * len(ATTN_GROUPS)
ATTN_WIDTH = ATTN_HEADS * HEAD_DIM
ATTN_BLOCK = 128
ATTN_BPI = 4
ROPE_THETA = 500000.0
ROPE_DIM = HEAD_DIM // 4
DN_HEADS = 16
DN_DIM = 128
DN_WIDTH = DN_HEADS * DN_DIM
DN_CONV = 4
DN_CHUNK = 64
DN_HB = 2
DN_CPI = 4
PEER_HEADS = 8
PEER_KEYS = 128
PEER_QDIM = 128
PEER_TOPK = 16
PEER_SLOTS = PEER_HEADS * PEER_TOPK
PEER_TB = 128
PEER_NBUF = 8
PEER_PITCH = 17

COL_GA = 0
COL_GD = COL_GA + 2048
COL_DQ = COL_GD + 2048
COL_DK = COL_DQ + DN_WIDTH
COL_DV = COL_DK + DN_WIDTH
COL_DZ = COL_DV + DN_WIDTH
COL_AQ = COL_DZ + DN_WIDTH
COL_AK = COL_AQ + ATTN_WIDTH
COL_AV = COL_AK + ATTN_WIDTH
COL_BA = COL_AV + ATTN_WIDTH
PROJ_WIDTH = 17 * 1024


def _silu(x):
    return x * jax.nn.sigmoid(x)


def _gelu_exact(x):
    return 0.5 * x * (1.0 + lax.erf(x * (2.0 ** -0.5)))


def _dot(a, b, precision=None):
    return jnp.dot(a, b, preferred_element_type=F32, precision=precision)


def _dot_nt(a, b, precision=None):
    return lax.dot_general(a, b, (((1,), (1,)), ((), ())), preferred_element_type=F32, precision=precision)


def _dot_tn(a, b, precision=None):
    return lax.dot_general(a, b, (((0,), (0,)), ((), ())), preferred_element_type=F32, precision=precision)


def _params(*sem):
    return pltpu.CompilerParams(dimension_semantics=sem, vmem_limit_bytes=VMEM_LIMIT)


def _ada_kernel(c_ref, w_ref, b_ref, o_ref):
    o_ref[...] = _dot(_silu(c_ref[...]), w_ref[...], HIGHEST) + b_ref[...]


def _ada(c, w, b):
    bsz, d = c.shape
    n = w.shape[1]
    tn = 1536
    return pl.pallas_call(
        _ada_kernel,
        grid=(n // tn,),
        in_specs=[pl.BlockSpec((bsz, d), lambda j: (0, 0)),
                  pl.BlockSpec((d, tn), lambda j: (0, j)),
                  pl.BlockSpec((1, tn), lambda j: (0, j))],
        out_specs=pl.BlockSpec((bsz, tn), lambda j: (0, j)),
        out_shape=jax.ShapeDtypeStruct((bsz, n), F32),
        compiler_params=_params("parallel"),
        name="ada",
    )(c, w, b.reshape(1, n))


def _rope_kernel(cos_ref, sin_ref):
    rows = cos_ref.shape[0]
    pos = (lax.broadcasted_iota(jnp.int32, (rows, LANES), 0) + pl.program_id(0) * rows).astype(F32)
    lane = lax.broadcasted_iota(jnp.int32, (rows, LANES), 1)
    half = ROPE_DIM // 2
    inv_freq = jnp.exp((lane % half).astype(F32) * (-(2.0 / ROPE_DIM) * math.log(ROPE_THETA)))
    ang = pos * inv_freq
    cos_ref[...] = jnp.where(lane < ROPE_DIM, jnp.cos(ang), 1.0)
    sin = jnp.sin(ang)
    sin_ref[...] = jnp.where(lane < half, -sin, jnp.where(lane < ROPE_DIM, sin, 0.0))


def _rope_tables(seq):
    rows = 256
    return pl.pallas_call(
        _rope_kernel,
        grid=(seq // rows,),
        out_specs=[pl.BlockSpec((rows, LANES), lambda i: (i, 0))] * 2,
        out_shape=[jax.ShapeDtypeStruct((seq, LANES), F32)] * 2,
        compiler_params=_params("parallel"),
        name="rope",
    )()


def _inproj_kernel(x_ref, ada_ref, nw_ref, w_ref, o_ref, h_ref):
    @pl.when(pl.program_id(1) == 0)
    def _():
        x = x_ref[...]
        y = x * lax.rsqrt(jnp.mean(x * x, axis=-1, keepdims=True) + NORM_EPS) * nw_ref[...]
        h_ref[...] = (y * (1.0 + ada_ref[1:2, :]) + ada_ref[0:1, :]).astype(BF16)

    o_ref[...] = _dot(h_ref[...], w_ref[...])


def _inproj(x2, ada3, norm_w, w_bf16, seq):
    t, d = x2.shape
    n = w_bf16.shape[1]
    tm, tn = 1024, 1024
    return pl.pallas_call(
        _inproj_kernel,
        grid=(t // tm, n // tn),
        in_specs=[pl.BlockSpec((tm, d), lambda i, j: (i, 0)),
                  pl.BlockSpec((None, 6, d), lambda i, j: (i * tm // seq, 0, 0)),
                  pl.BlockSpec((1, d), lambda i, j: (0, 0)),
                  pl.BlockSpec((d, tn), lambda i, j: (0, j))],
        out_specs=pl.BlockSpec((tm, tn), lambda i, j: (i, j)),
        out_shape=jax.ShapeDtypeStruct((t, n), F32),
        scratch_shapes=[pltpu.VMEM((tm, d), BF16)],
        compiler_params=_params("parallel", "arbitrary"),
        name="in_proj",
    )(x2, ada3, norm_w, w_bf16)


def _attn_kernel(q1, k1, v1, q2, k2, v2, q3, k3, v3, cos_ref, sin_ref, o_ref, qs, ks, vs, acc_s, m_s, l_s):
    seq = q1.shape[0]
    nblk = seq // ATTN_BLOCK
    half = ROPE_DIM // 2
    pr = lax.broadcasted_iota(jnp.int32, (LANES, LANES), 0)
    pc = lax.broadcasted_iota(jnp.int32, (LANES, LANES), 1)
    perm = (((pr == pc + half) & (pc < half)) | ((pr == pc - half) & (pc >= half) & (pc < ROPE_DIM))).astype(BF16)
    perm2 = jnp.concatenate([perm, perm], axis=0)

    def rotary(t):
        hi, lo = _bf16_parts(t, 2)
        partner = _dot(jnp.concatenate([hi, lo], axis=1).astype(BF16), perm2)
        return t * cos_ref[...] + partner * sin_ref[...]

    qi = lax.broadcasted_iota(jnp.int32, (ATTN_BLOCK, 2 * ATTN_BLOCK), 0)
    kj = lax.broadcasted_iota(jnp.int32, (ATTN_BLOCK, 2 * ATTN_BLOCK), 1)
    dist = qi + ATTN_BLOCK - kj
    scale = HEAD_DIM ** -0.5

    for g, (refs, (window, dil)) in enumerate(zip(((q1, k1, v1), (q2, k2, v2), (q3, k3, v3)), ATTN_GROUPS)):
        q_ref, k_ref, v_ref = refs
        n_back = window // dil
        pad = ATTN_BLOCK * dil
        in_window = (dist >= 0) & (dist <= n_back)
        qs[...] = rotary(q_ref[...])
        ks[0:pad, :] = jnp.zeros((pad, LANES), F32)
        vs[0:pad, :] = jnp.zeros((pad, LANES), F32)
        ks[pad:pad + seq, :] = rotary(k_ref[...])
        vs[pad:pad + seq, :] = v_ref[...]

        def rows_of(idx, dil=dil, pad=pad):
            res = lax.rem(idx, dil)
            nb = idx // dil
            start = res + nb * pad
            if dil == 1:
                return nb, pl.ds(start, ATTN_BLOCK), pl.ds(start, 2 * ATTN_BLOCK)
            return nb, pl.ds(start, ATTN_BLOCK, stride=dil), pl.ds(start, 2 * ATTN_BLOCK, stride=dil)

        def blocks(i, carry, g=g, rows_of=rows_of, in_window=in_window):
            rows = [rows_of(i * ATTN_BPI + j) for j in range(ATTN_BPI)]
            ss = [_dot_nt(qs[rq, :], ks[rk, :]) * scale for _, rq, rk in rows]
            ps, ms = [], []
            for (nb, _, _), s in zip(rows, ss):
                s = jnp.where(in_window & ((nb > 0) | (kj >= ATTN_BLOCK)), s, -jnp.inf)
                m = jnp.max(s, axis=-1, keepdims=True)
                ms.append(m)
                ps.append(jnp.exp(s - m))
            accs = [_dot(p, vs[rk, :]) for (_, _, rk), p in zip(rows, ps)]
            for (_, rq, _), acc, m, p in zip(rows, accs, ms, ps):
                acc_s[g, rq, :] = acc
                m_s[g, rq, :] = jnp.broadcast_to(m, (ATTN_BLOCK, LANES))
                l_s[g, rq, :] = jnp.broadcast_to(jnp.sum(p, axis=-1, keepdims=True), (ATTN_BLOCK, LANES))
            return carry

        lax.fori_loop(0, nblk // ATTN_BPI, blocks, 0)

    def merge(i, carry):
        rows = pl.ds(pl.multiple_of(i * ATTN_BLOCK, ATTN_BLOCK), ATTN_BLOCK)
        ms = [m_s[g, rows, :] for g in range(3)]
        mx = jnp.maximum(jnp.maximum(ms[0], ms[1]), ms[2])
        ws = [jnp.exp(m - mx) for m in ms]
        num = ws[0] * acc_s[0, rows, :] + ws[1] * acc_s[1, rows, :] + ws[2] * acc_s[2, rows, :]
        den = ws[0] * l_s[0, rows, :] + ws[1] * l_s[1, rows, :] + ws[2] * l_s[2, rows, :]
        o_ref[rows, :] = (num / den).astype(o_ref.dtype)
        return carry

    lax.fori_loop(0, nblk, merge, 0)


def _attention(proj, cos_t, sin_t, bsz, seq):
    for window, dil in ATTN_GROUPS:
        assert window // dil == ATTN_BLOCK and seq % (ATTN_BLOCK * dil) == 0
    pad_max = ATTN_BLOCK * max(d for _, d in ATTN_GROUPS)

    def head_spec(col0, g):
        blk0 = col0 // HEAD_DIM + g * ATTN_HPG
        return pl.BlockSpec((seq, HEAD_DIM), lambda b, h: (b, blk0 + h))

    in_specs = []
    for g in range(len(ATTN_GROUPS)):
        in_specs += [head_spec(COL_AQ, g), head_spec(COL_AK, g), head_spec(COL_AV, g)]
    in_specs += [pl.BlockSpec((seq, LANES), lambda b, h: (0, 0))] * 2
    return pl.pallas_call(
        _attn_kernel,
        grid=(bsz, ATTN_HPG),
        in_specs=in_specs,
        out_specs=pl.BlockSpec((seq, HEAD_DIM), lambda b, h: (b, h)),
        out_shape=jax.ShapeDtypeStruct((bsz * seq, ATTN_HPG * HEAD_DIM), BF16),
        scratch_shapes=[pltpu.VMEM((seq, LANES), F32),
                        pltpu.VMEM((seq + pad_max, LANES), F32),
                        pltpu.VMEM((seq + pad_max, LANES), F32),
                        pltpu.VMEM((3, seq, LANES), F32),
                        pltpu.VMEM((3, seq, LANES), F32),
                        pltpu.VMEM((3, seq, LANES), F32)],
        compiler_params=_params("parallel", "parallel"),
        name="attention",
    )(*([proj] * 9), cos_t, sin_t)


def _bf16_parts(a, n):
    parts, rest = [], a
    for _ in range(n):
        piece = rest.astype(BF16).astype(F32)
        parts.append(piece)
        rest = rest - piece
    return parts


def _lhs3(a):
    hi, lo = _bf16_parts(a, 2)
    return jnp.concatenate([hi, hi, lo], axis=1).astype(BF16)


def _rhs3(b):
    hi, lo = _bf16_parts(b, 2)
    return jnp.concatenate([hi, lo, hi], axis=0).astype(BF16)


def _split3_rows(b):
    return jnp.concatenate(_bf16_parts(b, 3), axis=0).astype(BF16)


def _deltanet_kernel(q_ref, k_ref, v_ref, z_ref, ba_ref, cwq_ref, cwk_ref, cwv_ref, alog_ref, dtb_ref, nw_ref,
                     o_ref, u_s, wq_s, qkkd_s, cd_s):
    seq = q_ref.shape[0]
    nchunk = seq // DN_CHUNK
    c64 = DN_CHUNK
    ri = lax.broadcasted_iota(jnp.int32, (c64, c64), 0)
    ci = lax.broadcasted_iota(jnp.int32, (c64, c64), 1)
    lower_incl = ri >= ci
    lower_strict = ri > ci
    ones_lower = lower_incl.astype(F32)
    eye = (ri == ci).astype(F32)
    lane = lax.broadcasted_iota(jnp.int32, (c64, LANES), 1)
    lane16 = lax.broadcasted_iota(jnp.int32, (1, DN_HEADS), 1)

    eye_bf16 = eye.astype(BF16)
    ones3 = jnp.concatenate([ones_lower] * 3, axis=1).astype(BF16)

    def conv_silu(x_ref, cw_ref, hh, r0, c):
        cols = slice(hh * DN_DIM, (hh + 1) * DN_DIM)
        prev = x_ref[pl.ds(pl.multiple_of(jnp.maximum(r0 - SUBLANES, 0), SUBLANES), SUBLANES), cols]
        prev = jnp.where(c > 0, prev, 0.0)
        win = jnp.concatenate([prev, x_ref[pl.ds(r0, c64), cols]], axis=0)
        y = win[SUBLANES:, :] * cw_ref[DN_CONV - 1:DN_CONV, cols]
        for j in range(1, DN_CONV):
            y = y + pltpu.roll(win, j, 0)[SUBLANES:, :] * cw_ref[DN_CONV - 1 - j:DN_CONV - j, cols]
        return _silu(y)

    def prepare(c, hh):
        r0 = pl.multiple_of(c * c64, c64)
        head = pl.program_id(1) * DN_HB + hh
        qx = conv_silu(q_ref, cwq_ref, hh, r0, c)
        kx = conv_silu(k_ref, cwk_ref, hh, r0, c)
        vx = conv_silu(v_ref, cwv_ref, hh, r0, c)
        qn = qx * lax.rsqrt(jnp.sum(qx * qx, axis=-1, keepdims=True) + 1e-6) * (DN_DIM ** -0.5)
        kn = kx * lax.rsqrt(jnp.sum(kx * kx, axis=-1, keepdims=True) + 1e-6)
        ba = ba_ref[pl.ds(r0, c64), :]
        b_col = jnp.sum(jnp.where(lane == head, ba, 0.0), axis=-1, keepdims=True)
        a_col = jnp.sum(jnp.where(lane == head + DN_HEADS, ba, 0.0), axis=-1, keepdims=True)
        a_log = jnp.sum(jnp.where(lane16 == head, alog_ref[...], 0.0), axis=-1, keepdims=True)
        dt_b = jnp.sum(jnp.where(lane16 == head, dtb_ref[...], 0.0), axis=-1, keepdims=True)
        beta = jax.nn.sigmoid(b_col)
        ax = a_col + dt_b
        softplus = jnp.maximum(ax, 0.0) + jnp.log1p(jnp.exp(-jnp.abs(ax)))
        g = -jnp.exp(a_log) * softplus
        gm = jnp.where(lower_strict, jnp.broadcast_to(g, (c64, c64)), 0.0)
        kb = kn * beta
        return dict(qn=qn, kn=kn, vb=vx * beta, kb=kb, g=g, gm3=_split3_rows(gm),
                    kbqn=jnp.concatenate([kb, qn], axis=0).astype(BF16), kn16=kn.astype(BF16))

    def local(i, carry):
        todo = [(i * DN_CPI + j, hh) for j in range(DN_CPI) for hh in range(DN_HB)]
        ps = [prepare(c, hh) for c, hh in todo]
        diffs = [_dot(ones3, p["gm3"]) for p in ps]
        scores = [_dot_nt(p["kbqn"], p["kn16"]) for p in ps]
        mids = []
        for p, diff, sc in zip(ps, diffs, scores):
            gam = diff[:, 0:1] + p["g"][0:1, :]
            gam_last = gam[c64 - 1:c64, :]
            decay = jnp.where(lower_incl, jnp.exp(jnp.where(lower_incl, diff, 0.0)), 0.0)
            eg = jnp.exp(gam)
            mids.append(dict(nmat=-jnp.where(lower_strict, sc[:c64] * decay, 0.0), qk=sc[c64:] * decay,
                             rhs=jnp.concatenate([p["vb"], p["kb"] * eg], axis=-1), qd=p["qn"] * eg,
                             kd=(p["kn"] * jnp.exp(gam_last - gam)).astype(BF16),
                             cd=jnp.broadcast_to(jnp.exp(gam_last), (SUBLANES, LANES))))
        invs = [eye + m["nmat"] for m in mids]
        pws = [_dot(_lhs3(m["nmat"]), _rhs3(m["nmat"])) for m in mids]
        for _ in range(4):
            boths = [_dot(_lhs3(pw), _rhs3(jnp.concatenate([inv, pw], axis=1))) for inv, pw in zip(invs, pws)]
            invs = [inv + b[:, :c64] for inv, b in zip(invs, boths)]
            pws = [b[:, c64:] for b in boths]
        ys = [_dot(_lhs3(inv), _rhs3(m["rhs"])) for inv, m in zip(invs, mids)]
        sols = [y + _dot(_lhs3(pw), _rhs3(y)) for y, pw in zip(ys, pws)]
        kdts = [_dot_tn(m["kd"], eye_bf16) for m in mids]
        for (c, hh), m, sol, kd_t in zip(todo, mids, sols, kdts):
            u_s[hh, pl.ds(pl.multiple_of(c * c64, c64), c64), :] = sol[:, :DN_DIM]
            wq_s[hh, pl.ds(pl.multiple_of(c * (2 * c64), 2 * c64), 2 * c64), :] = jnp.concatenate(
                [sol[:, DN_DIM:], m["qd"]], axis=0).astype(BF16)
            qkkd_s[hh, pl.ds(pl.multiple_of(c * (c64 + DN_DIM), c64), c64 + DN_DIM), :] = jnp.concatenate(
                [m["qk"], kd_t], axis=0).astype(BF16)
            cd_s[hh, pl.ds(pl.multiple_of(c * SUBLANES, SUBLANES), SUBLANES), :] = m["cd"]
        return carry

    lax.fori_loop(0, nchunk // DN_CPI, local, 0)

    def scan(c, states):
        r0 = pl.multiple_of(c * c64, c64)
        rows = pl.ds(r0, c64)
        heads = range(DN_HB)
        r1 = pl.multiple_of(c * (2 * c64), 2 * c64)
        r2 = pl.multiple_of(c * (c64 + DN_DIM), c64)
        cds = [cd_s[hh, pl.ds(pl.multiple_of(c * SUBLANES, SUBLANES), SUBLANES), :][0:1, :] for hh in heads]
        wss = [_dot(wq_s[hh, pl.ds(r1, 2 * c64), :], states[hh].astype(BF16)) for hh in heads]
        v_news = [u_s[hh, rows, :] - wss[hh][:c64] for hh in heads]
        oss = [_dot(qkkd_s[hh, pl.ds(r2, c64 + DN_DIM), :], v_news[hh].astype(BF16)) for hh in heads]
        new_states = [states[hh] * cds[hh] + oss[hh][c64:] for hh in heads]
        outs = []
        for hh in heads:
            o = wss[hh][c64:] + oss[hh][:c64]
            on = o * lax.rsqrt(jnp.mean(o * o, axis=-1, keepdims=True) + NORM_EPS) * nw_ref[...]
            outs.append((on * _silu(z_ref[rows, hh * DN_DIM:(hh + 1) * DN_DIM])).astype(o_ref.dtype))
        for hh in heads:
            o_ref[rows, hh * DN_DIM:(hh + 1) * DN_DIM] = outs[hh]
        return tuple(new_states)

    lax.fori_loop(0, nchunk, scan, tuple(jnp.zeros((DN_DIM, DN_DIM), F32) for _ in range(DN_HB)))


def _deltanet(proj, conv_w, a_log, dt_bias, norm_w, bsz, seq):
    wb = DN_DIM * DN_HB
    hblocks = DN_HEADS // DN_HB
    nchunk = seq // DN_CHUNK
    assert nchunk % DN_CPI == 0

    def act_spec(col0):
        blk0 = col0 // wb
        return pl.BlockSpec((seq, wb), lambda b, h: (b, blk0 + h))

    def cw_spec(part):
        blk0 = part * DN_WIDTH // wb
        return pl.BlockSpec((DN_CONV, wb), lambda b, h: (0, blk0 + h))

    return pl.pallas_call(
        _deltanet_kernel,
        grid=(bsz, hblocks),
        in_specs=[act_spec(COL_DQ), act_spec(COL_DK), act_spec(COL_DV), act_spec(COL_DZ),
                  pl.BlockSpec((seq, LANES), lambda b, h: (b, COL_BA // LANES)),
                  cw_spec(0), cw_spec(1), cw_spec(2),
                  pl.BlockSpec((1, DN_HEADS), lambda b, h: (0, 0)),
                  pl.BlockSpec((1, DN_HEADS), lambda b, h: (0, 0)),
                  pl.BlockSpec((1, DN_DIM), lambda b, h: (0, 0))],
        out_specs=pl.BlockSpec((seq, wb), lambda b, h: (b, h)),
        out_shape=jax.ShapeDtypeStruct((bsz * seq, DN_WIDTH), BF16),
        scratch_shapes=[pltpu.VMEM((DN_HB, seq, DN_DIM), F32),
                        pltpu.VMEM((DN_HB, nchunk * 2 * DN_CHUNK, DN_DIM), BF16),
                        pltpu.VMEM((DN_HB, nchunk * (DN_CHUNK + DN_DIM), DN_CHUNK), BF16),
                        pltpu.VMEM((DN_HB, nchunk * SUBLANES, LANES), F32)],
        compiler_params=_params("parallel", "parallel"),
        name="deltanet",
    )(proj, proj, proj, proj, proj, conv_w, conv_w, conv_w,
      a_log.reshape(1, DN_HEADS), dt_bias.reshape(1, DN_HEADS), norm_w.reshape(1, DN_DIM))


def _mixout_kernel(ya_ref, og_ref, ga_ref, gd_ref, x_ref, ada_ref, wa_ref, wd_ref, wm_ref, npost_ref, npre_ref,
                   xmid_ref, h2_ref):
    y_attn = _dot(ya_ref[...], wa_ref[...])
    y_delta = _dot(og_ref[...], wd_ref[...])
    merged = jax.nn.sigmoid(ga_ref[...]) * y_attn + jax.nn.sigmoid(gd_ref[...]) * y_delta
    y = _dot(merged.astype(BF16), wm_ref[...])
    yn = y * lax.rsqrt(jnp.mean(y * y, axis=-1, keepdims=True) + NORM_EPS) * npost_ref[...]
    x_mid = x_ref[...] + ada_ref[2:3, :] * yn
    xmid_ref[...] = x_mid
    hn = x_mid * lax.rsqrt(jnp.mean(x_mid * x_mid, axis=-1, keepdims=True) + NORM_EPS) * npre_ref[...]
    h2_ref[...] = hn * (1.0 + ada_ref[4:5, :]) + ada_ref[3:4, :]


def _mixout(y_attn, o_gated, proj, x2, ada3, w_attn, w_delta, w_mix, norm_post, norm_pre_ffn, seq):
    t, d = x2.shape
    tm = 256
    const = dict(pipeline_mode=pl.Buffered(1))
    return pl.pallas_call(
        _mixout_kernel,
        grid=(t // tm,),
        in_specs=[pl.BlockSpec((tm, y_attn.shape[1]), lambda i: (i, 0)),
                  pl.BlockSpec((tm, d), lambda i: (i, 0)),
                  pl.BlockSpec((tm, d), lambda i: (i, COL_GA // d)),
                  pl.BlockSpec((tm, d), lambda i: (i, COL_GD // d)),
                  pl.BlockSpec((tm, d), lambda i: (i, 0)),
                  pl.BlockSpec((None, 6, d), lambda i: (i * tm // seq, 0, 0)),
                  pl.BlockSpec(w_attn.shape, lambda i: (0, 0), **const),
                  pl.BlockSpec(w_delta.shape, lambda i: (0, 0), **const),
                  pl.BlockSpec(w_mix.shape, lambda i: (0, 0), **const),
                  pl.BlockSpec((1, d), lambda i: (0, 0)),
                  pl.BlockSpec((1, d), lambda i: (0, 0))],
        out_specs=[pl.BlockSpec((tm, d), lambda i: (i, 0))] * 2,
        out_shape=[jax.ShapeDtypeStruct((t, d), F32)] * 2,
        compiler_params=_params("parallel"),
        name="mix_out",
    )(y_attn, o_gated, proj, proj, x2, ada3, w_attn, w_delta, w_mix, norm_post, norm_pre_ffn)


def _topk_rows(s, k):
    n = s.shape[0]
    rid = lax.broadcasted_iota(jnp.int32, s.shape, 0)
    vals, ids = [], []
    for _ in range(k):
        m = jnp.max(s, axis=0, keepdims=True)
        sel = jnp.min(jnp.where(s == m, rid, n), axis=0, keepdims=True)
        vals.append(m)
        ids.append(sel)
        s = jnp.where(rid == sel, -jnp.inf, s)
    return jnp.concatenate(vals, axis=0), jnp.concatenate(ids, axis=0)


def _peer_route_kernel(h_ref, wq_ref, keys_ref, idx_ref, gw_ref, q_s, idx_s, gw_s):
    tm = h_ref.shape[0]
    head = pl.program_id(1)
    half = PEER_QDIM // 2

    @pl.when(head == 0)
    def _():
        q = _dot(h_ref[...].astype(BF16), wq_ref[...])
        for hh in range(PEER_HEADS):
            q_s[hh] = q[:, hh * PEER_QDIM:(hh + 1) * PEER_QDIM]

    qh = q_s[head]
    tops = []
    for p in range(2):
        s = _dot_nt(keys_ref[0, p], qh[:, p * half:(p + 1) * half], HIGHEST)
        tops.append(_topk_rows(s, PEER_TOPK))
    (s0, i0), (s1, i1) = tops
    counts = [PEER_TOPK // (a + 1) for a in range(PEER_TOPK)]
    n_cand = sum(counts)
    n_pad = -n_cand % SUBLANES
    cand = jnp.concatenate([s0[a:a + 1, :] + s1[0:counts[a], :] for a in range(PEER_TOPK)]
                           + [jnp.full((n_pad, tm), -jnp.inf, F32)], axis=0)
    best_s, best_row = _topk_rows(cand, PEER_TOPK)
    a_sel = jnp.zeros_like(best_row)
    b_sel = best_row
    start = 0
    for a in range(PEER_TOPK - 1):
        start += counts[a]
        past = (best_row >= start).astype(jnp.int32)
        a_sel = a_sel + past
        b_sel = b_sel - past * counts[a]
    e0 = jnp.zeros_like(best_row)
    e1 = jnp.zeros_like(best_row)
    for a in range(PEER_TOPK):
        e0 = e0 + jnp.where(a_sel == a, i0[a:a + 1, :], 0)
        e1 = e1 + jnp.where(b_sel == a, i1[a:a + 1, :], 0)
    ex = jnp.exp(best_s - jnp.max(best_s, axis=0, keepdims=True))
    rows = pl.ds(pl.multiple_of(head * PEER_TOPK, PEER_TOPK), PEER_TOPK)
    idx_s[rows, :] = e0 * PEER_KEYS + e1
    gw_s[rows, :] = ex / jnp.sum(ex, axis=0, keepdims=True)

    @pl.when(head == PEER_HEADS - 1)
    def _():
        for bb in range(tm // PEER_TB):
            tok = slice(bb * PEER_TB, (bb + 1) * PEER_TB)
            gw_ref[bb] = gw_s[:, tok]
            idx_ref[bb] = idx_s[:, tok].T


def _peer_route(h2, w_query_bf16, keys):
    t, d = h2.shape
    tm = 256
    nb = tm // PEER_TB
    return pl.pallas_call(
        _peer_route_kernel,
        grid=(t // tm, PEER_HEADS),
        in_specs=[pl.BlockSpec((tm, d), lambda i, h: (i, 0)),
                  pl.BlockSpec(w_query_bf16.shape, lambda i, h: (0, 0)),
                  pl.BlockSpec((1, 2, PEER_KEYS, PEER_QDIM // 2), lambda i, h: (h, 0, 0, 0))],
        out_specs=[pl.BlockSpec((nb, PEER_TB, PEER_SLOTS), lambda i, h: (i, 0, 0)),
                   pl.BlockSpec((nb, PEER_SLOTS, PEER_TB), lambda i, h: (i, 0, 0))],
        out_shape=[jax.ShapeDtypeStruct((t // PEER_TB, PEER_TB, PEER_SLOTS), jnp.int32),
                   jax.ShapeDtypeStruct((t // PEER_TB, PEER_SLOTS, PEER_TB), F32)],
        scratch_shapes=[pltpu.VMEM((PEER_HEADS, tm, PEER_QDIM), F32),
                        pltpu.VMEM((PEER_SLOTS, tm), jnp.int32),
                        pltpu.VMEM((PEER_SLOTS, tm), F32)],
        compiler_params=_params("parallel", "arbitrary"),
        name="peer_route",
    )(h2, w_query_bf16, keys)


def _peer_gather_kernel(idx_ref, gw_ref, h_ref, xmid_ref, ada_ref, nw_ref, down_hbm, up_hbm,
                        out_ref, buf, sem, hmat, ybuf):
    tb, d = h_ref.shape
    nsub = d // LANES
    slot_rows = PEER_SLOTS * PEER_PITCH
    nbuf = PEER_NBUF
    lane = lax.broadcasted_iota(jnp.int32, (PEER_SLOTS, tb), 1)

    def row_copy(tbl, row, slot, k):
        return pltpu.make_async_copy(tbl.at[row], buf.at[pl.ds(slot * slot_rows + k * PEER_PITCH, nsub)],
                                     sem.at[slot])

    def issue(tbl, t, slot):
        for k in range(PEER_SLOTS):
            row_copy(tbl, idx_ref[0, t, k], slot, k).start(priority=k % 2)

    def wait(tbl, slot):
        rows = pl.ds(slot * slot_rows, PEER_SLOTS * nsub)
        pltpu.make_async_copy(buf.at[rows], buf.at[rows], sem.at[slot]).wait()

    def tile(slot, s):
        return buf[pl.ds(slot * slot_rows + s, PEER_SLOTS, stride=PEER_PITCH), :]

    def run_phase(tbl, body):
        for j in range(nbuf):
            issue(tbl, j, j)

        def group(g, carry):
            for j in range(nbuf):
                t = g * nbuf + j
                wait(tbl, j)
                body(t, j)
                issue(tbl, t + nbuf, j)
            return carry

        lax.fori_loop(0, tb // nbuf - 1, group, 0)
        for j in range(nbuf):
            wait(tbl, j)
            body(tb - nbuf + j, j)

    def body_down(t, slot):
        xrow = h_ref[pl.ds(t, 1), :]
        acc = tile(slot, 0) * xrow[:, 0:LANES]
        for s in range(1, nsub):
            acc = acc + tile(slot, s) * xrow[:, s * LANES:(s + 1) * LANES]
        hcol = jnp.sum(acc, axis=-1, keepdims=True)
        hmat[...] = jnp.where(lane == t, hcol, hmat[...])

    hmat[...] = jnp.zeros_like(hmat)
    run_phase(down_hbm, body_down)
    hmat[...] = gw_ref[0] * _gelu_exact(hmat[...])

    def body_up(t, slot):
        ccol = jnp.sum(jnp.where(lane == t, hmat[...], 0.0), axis=-1, keepdims=True)
        parts = [jnp.sum(tile(slot, s) * ccol, axis=0, keepdims=True) for s in range(nsub)]
        ybuf[pl.ds(t, 1), :] = jnp.concatenate(parts, axis=-1)

    run_phase(up_hbm, body_up)
    y = ybuf[...]
    yn = y * lax.rsqrt(jnp.mean(y * y, axis=-1, keepdims=True) + NORM_EPS) * nw_ref[...]
    out_ref[...] = xmid_ref[...] + ada_ref[5:6, :] * yn


def _peer_gather(idx_t, gw_t, h2, x_mid, ada3, norm_w, down3, up3, seq):
    t, d = h2.shape
    blocks_per_batch = seq // PEER_TB
    assert PEER_TB % PEER_NBUF == 0 and d // LANES < PEER_PITCH
    return pl.pallas_call(
        _peer_gather_kernel,
        grid=(t // PEER_TB,),
        in_specs=[pl.BlockSpec((1, PEER_TB, PEER_SLOTS), lambda i: (i, 0, 0), memory_space=pltpu.SMEM),
                  pl.BlockSpec((1, PEER_SLOTS, PEER_TB), lambda i: (i, 0, 0)),
                  pl.BlockSpec((PEER_TB, d), lambda i: (i, 0)),
                  pl.BlockSpec((PEER_TB, d), lambda i: (i, 0)),
                  pl.BlockSpec((None, 6, d), lambda i: (i // blocks_per_batch, 0, 0)),
                  pl.BlockSpec((1, d), lambda i: (0, 0)),
                  pl.BlockSpec(memory_space=pl.ANY),
                  pl.BlockSpec(memory_space=pl.ANY)],
        out_specs=pl.BlockSpec((PEER_TB, d), lambda i: (i, 0)),
        out_shape=jax.ShapeDtypeStruct((t, d), F32),
        scratch_shapes=[pltpu.VMEM((PEER_NBUF * PEER_SLOTS * PEER_PITCH, LANES), F32),
                        pltpu.SemaphoreType.DMA((PEER_NBUF,)),
                        pltpu.VMEM((PEER_SLOTS, PEER_TB), F32),
                        pltpu.VMEM((PEER_TB, d), F32)],
        compiler_params=_params("arbitrary"),
        name="peer_gather",
    )(idx_t, gw_t, h2, x_mid, ada3, norm_w, down3, up3)


def _permute_w_in(w_in):
    o_attn, o_dn, o_z = 0, 3 * ATTN_WIDTH, 3 * ATTN_WIDTH + 3 * DN_WIDTH
    o_b = o_z + DN_WIDTH
    o_gates = o_b + 2 * DN_HEADS
    parts = [w_in[:, o_gates:o_gates + 4096], w_in[:, o_dn:o_dn + 3 * DN_WIDTH], w_in[:, o_z:o_z + DN_WIDTH],
             w_in[:, o_attn:o_attn + 3 * ATTN_WIDTH], w_in[:, o_b:o_b + 2 * DN_HEADS]]
    w = jnp.concatenate(parts, axis=1)
    return jnp.pad(w, ((0, 0), (0, PROJ_WIDTH - w.shape[1]))).astype(BF16)


def _layer(x2, c, bsz, seq, w_ada, b_ada, norm_pre_mix, norm_post_mix, norm_pre_ffn, norm_post_ffn, w_in, conv_w,
           a_log, dt_bias, dn_norm_w, w_attn_out, w_delta_out, w_mix_out, peer_w_query, peer_sub_keys, peer_down,
           peer_up, cos_t, sin_t):
    d = x2.shape[1]
    row = lambda v: v.reshape(1, -1)
    ada3 = _ada(c, w_ada, b_ada).reshape(bsz, 6, d)
    proj = _inproj(x2, ada3, row(norm_pre_mix), _permute_w_in(w_in), seq)
    y_attn = _attention(proj, cos_t, sin_t, bsz, seq)
    o_gated = _deltanet(proj, conv_w, a_log, dt_bias, dn_norm_w, bsz, seq)
    x_mid, h2 = _mixout(y_attn, o_gated, proj, x2, ada3, w_attn_out.astype(BF16), w_delta_out.astype(BF16),
                        w_mix_out.astype(BF16), row(norm_post_mix), row(norm_pre_ffn), seq)
    idx_t, gw_t = _peer_route(h2, peer_w_query.astype(BF16), peer_sub_keys)
    nsub = d // LANES
    down3 = peer_down.reshape(peer_down.shape[0], nsub, LANES)
    up3 = peer_up.reshape(peer_up.shape[0], nsub, LANES)
    return _peer_gather(idx_t, gw_t, h2, x_mid, ada3, row(norm_post_ffn), down3, up3, seq)


def kernel(x, c, w_ada, b_ada, norm_pre_mix, norm_post_mix, norm_pre_ffn, norm_post_ffn, w_in, conv_w, a_log, dt_bias, dn_norm_w, w_attn_out, w_delta_out, w_mix_out, peer_w_query, peer_sub_keys, peer_down, peer_up):
    bsz, seq, d = x.shape
    x2 = x.reshape(bsz * seq, d)
    cos_t, sin_t = _rope_tables(seq)
    for layer in range(w_ada.shape[0]):
        x2 = _layer(x2, c, bsz, seq, w_ada[layer], b_ada[layer], norm_pre_mix[layer], norm_post_mix[layer],
                    norm_pre_ffn[layer], norm_post_ffn[layer], w_in[layer], conv_w[layer], a_log[layer],
                    dt_bias[layer], dn_norm_w[layer], w_attn_out[layer], w_delta_out[layer], w_mix_out[layer],
                    peer_w_query[layer], peer_sub_keys[layer], peer_down[layer], peer_up[layer], cos_t, sin_t)
    return x2.reshape(bsz, seq, d)
```

```python
import functools
import math

import jax
import jax.numpy as jnp
from jax import lax
from jax.experimental import pallas as pl
from jax.experimental.pallas import tpu as pltpu

F32 = jnp.float32
BF16 = jnp.bfloat16
HIGHEST = lax.Precision.HIGHEST
LANES = 128
SUBLANES = 8
VMEM_LIMIT = 56 * 1024 * 1024

NORM_EPS = 1e-6
HEAD_DIM = 128
ATTN_GROUPS = ((128, 1), (512, 4), (2048, 16))
ATTN_HPG = 4
ATTN_HEADS = ATTN_HPG * len(ATTN_GROUPS)
ATTN_WIDTH = ATTN_HEADS * HEAD_DIM
ATTN_BLOCK = 128
ATTN_BPI = 4
ROPE_THETA = 500000.0
ROPE_DIM = HEAD_DIM // 4
DN_HEADS = 16
DN_DIM = 128
DN_WIDTH = DN_HEADS * DN_DIM
DN_CONV = 4
DN_CHUNK = 64
DN_HB = 2
DN_CPI = 4
PEER_HEADS = 8
PEER_KEYS = 128
PEER_QDIM = 128
PEER_TOPK = 16
PEER_SLOTS = PEER_HEADS * PEER_TOPK
PEER_TB = 128
PEER_NBUF = 8
PEER_PITCH = 9

COL_GA = 0
COL_GD = COL_GA + 2048
COL_DQ = COL_GD + 2048
COL_DK = COL_DQ + DN_WIDTH
COL_DV = COL_DK + DN_WIDTH
COL_DZ = COL_DV + DN_WIDTH
COL_AQ = COL_DZ + DN_WIDTH
COL_AK = COL_AQ + ATTN_WIDTH
COL_AV = COL_AK + ATTN_WIDTH
COL_BA = COL_AV + ATTN_WIDTH
PROJ_WIDTH = 17 * 1024


def _silu(x):
    return x * jax.nn.sigmoid(x)


def _gelu_exact(x):
    return 0.5 * x * (1.0 + lax.erf(x * (2.0 ** -0.5)))


def _dot(a, b, precision=None):
    return jnp.dot(a, b, preferred_element_type=F32, precision=precision)


def _dot_nt(a, b, precision=None):
    return lax.dot_general(a, b, (((1,), (1,)), ((), ())), preferred_element_type=F32, precision=precision)


def _dot_tn(a, b, precision=None):
    return lax.dot_general(a, b, (((0,), (0,)), ((), ())), preferred_element_type=F32, precision=precision)


def _params(*sem):
    return pltpu.CompilerParams(dimension_semantics=sem, vmem_limit_bytes=VMEM_LIMIT)


def _ada_kernel(c_ref, w_ref, b_ref, o_ref):
    o_ref[...] = _dot(_silu(c_ref[...]), w_ref[...], HIGHEST) + b_ref[...]


def _ada(c, w, b):
    bsz, d = c.shape
    n = w.shape[1]
    tn = 1536
    return pl.pallas_call(
        _ada_kernel,
        grid=(n // tn,),
        in_specs=[pl.BlockSpec((bsz, d), lambda j: (0, 0)),
                  pl.BlockSpec((d, tn), lambda j: (0, j)),
                  pl.BlockSpec((1, tn), lambda j: (0, j))],
        out_specs=pl.BlockSpec((bsz, tn), lambda j: (0, j)),
        out_shape=jax.ShapeDtypeStruct((bsz, n), F32),
        compiler_params=_params("parallel"),
        name="ada",
    )(c, w, b.reshape(1, n))


def _rope_kernel(cos_ref, sin_ref):
    rows = cos_ref.shape[0]
    pos = (lax.broadcasted_iota(jnp.int32, (rows, LANES), 0) + pl.program_id(0) * rows).astype(F32)
    lane = lax.broadcasted_iota(jnp.int32, (rows, LANES), 1)
    half = ROPE_DIM // 2
    inv_freq = jnp.exp((lane % half).astype(F32) * (-(2.0 / ROPE_DIM) * math.log(ROPE_THETA)))
    ang = pos * inv_freq
    cos_ref[...] = jnp.where(lane < ROPE_DIM, jnp.cos(ang), 1.0)
    sin = jnp.sin(ang)
    sin_ref[...] = jnp.where(lane < half, -sin, jnp.where(lane < ROPE_DIM, sin, 0.0))


def _rope_tables(seq):
    rows = 256
    return pl.pallas_call(
        _rope_kernel,
        grid=(seq // rows,),
        out_specs=[pl.BlockSpec((rows, LANES), lambda i: (i, 0))] * 2,
        out_shape=[jax.ShapeDtypeStruct((seq, LANES), F32)] * 2,
        compiler_params=_params("parallel"),
        name="rope",
    )()


def _inproj_kernel(x_ref, ada_ref, nw_ref, w_ref, o_ref, h_ref):
    @pl.when(pl.program_id(1) == 0)
    def _():
        x = x_ref[...]
        y = x * lax.rsqrt(jnp.mean(x * x, axis=-1, keepdims=True) + NORM_EPS) * nw_ref[...]
        h_ref[...] = (y * (1.0 + ada_ref[1:2, :]) + ada_ref[0:1, :]).astype(BF16)

    o_ref[...] = _dot(h_ref[...], w_ref[...])


def _inproj(x2, ada3, norm_w, w_bf16, seq):
    t, d = x2.shape
    n = w_bf16.shape[1]
    tm, tn = 1024, 1024
    return pl.pallas_call(
        _inproj_kernel,
        grid=(t // tm, n // tn),
        in_specs=[pl.BlockSpec((tm, d), lambda i, j: (i, 0)),
                  pl.BlockSpec((None, 6, d), lambda i, j: (i * tm // seq, 0, 0)),
                  pl.BlockSpec((1, d), lambda i, j: (0, 0)),
                  pl.BlockSpec((d, tn), lambda i, j: (0, j))],
        out_specs=pl.BlockSpec((tm, tn), lambda i, j: (i, j)),
        out_shape=jax.ShapeDtypeStruct((t, n), F32),
        scratch_shapes=[pltpu.VMEM((tm, d), BF16)],
        compiler_params=_params("parallel", "arbitrary"),
        name="in_proj",
    )(x2, ada3, norm_w, w_bf16)


def _attn_kernel(q1, k1, v1, q2, k2, v2, q3, k3, v3, cos_ref, sin_ref, o_ref, qs, ks, vs, acc_s, m_s, l_s):
    seq = q1.shape[0]
    nblk = seq // ATTN_BLOCK
    half = ROPE_DIM // 2
    pr = lax.broadcasted_iota(jnp.int32, (LANES, LANES), 0)
    pc = lax.broadcasted_iota(jnp.int32, (LANES, LANES), 1)
    perm = (((pr == pc + half) & (pc < half)) | ((pr == pc - half) & (pc >= half) & (pc < ROPE_DIM))).astype(BF16)
    perm2 = jnp.concatenate([perm, perm], axis=0)

    def rotary(t):
        hi, lo = _bf16_parts(t, 2)
        partner = _dot(jnp.concatenate([hi, lo], axis=1).astype(BF16), perm2)
        return t * cos_ref[...] + partner * sin_ref[...]

    qi = lax.broadcasted_iota(jnp.int32, (ATTN_BLOCK, 2 * ATTN_BLOCK), 0)
    kj = lax.broadcasted_iota(jnp.int32, (ATTN_BLOCK, 2 * ATTN_BLOCK), 1)
    dist = qi + ATTN_BLOCK - kj
    scale = HEAD_DIM ** -0.5

    for g, (refs, (window, dil)) in enumerate(zip(((q1, k1, v1), (q2, k2, v2), (q3, k3, v3)), ATTN_GROUPS)):
        q_ref, k_ref, v_ref = refs
        n_back = window // dil
        pad = ATTN_BLOCK * dil
        in_window = (dist >= 0) & (dist <= n_back)
        qs[...] = rotary(q_ref[...])
        ks[0:pad, :] = jnp.zeros((pad, LANES), F32)
        vs[0:pad, :] = jnp.zeros((pad, LANES), F32)
        ks[pad:pad + seq, :] = rotary(k_ref[...])
        vs[pad:pad + seq, :] = v_ref[...]

        def rows_of(idx, dil=dil, pad=pad):
            res = lax.rem(idx, dil)
            nb = idx // dil
            start = res + nb * pad
            if dil == 1:
                return nb, pl.ds(start, ATTN_BLOCK), pl.ds(start, 2 * ATTN_BLOCK)
            return nb, pl.ds(start, ATTN_BLOCK, stride=dil), pl.ds(start, 2 * ATTN_BLOCK, stride=dil)

        def blocks(i, carry, g=g, rows_of=rows_of, in_window=in_window):
            rows = [rows_of(i * ATTN_BPI + j) for j in range(ATTN_BPI)]
            ss = [_dot_nt(qs[rq, :], ks[rk, :]) * scale for _, rq, rk in rows]
            ps, ms = [], []
            for (nb, _, _), s in zip(rows, ss):
                s = jnp.where(in_window & ((nb > 0) | (kj >= ATTN_BLOCK)), s, -jnp.inf)
                m = jnp.max(s, axis=-1, keepdims=True)
                ms.append(m)
                ps.append(jnp.exp(s - m))
            accs = [_dot(p, vs[rk, :]) for (_, _, rk), p in zip(rows, ps)]
            for (_, rq, _), acc, m, p in zip(rows, accs, ms, ps):
                acc_s[g, rq, :] = acc
                m_s[g, rq, :] = jnp.broadcast_to(m, (ATTN_BLOCK, LANES))
                l_s[g, rq, :] = jnp.broadcast_to(jnp.sum(p, axis=-1, keepdims=True), (ATTN_BLOCK, LANES))
            return carry

        lax.fori_loop(0, nblk // ATTN_BPI, blocks, 0)

    def merge(i, carry):
        rows = pl.ds(pl.multiple_of(i * ATTN_BLOCK, ATTN_BLOCK), ATTN_BLOCK)
        ms = [m_s[g, rows, :] for g in range(3)]
        mx = jnp.maximum(jnp.maximum(ms[0], ms[1]), ms[2])
        ws = [jnp.exp(m - mx) for m in ms]
        num = ws[0] * acc_s[0, rows, :] + ws[1] * acc_s[1, rows, :] + ws[2] * acc_s[2, rows, :]
        den = ws[0] * l_s[0, rows, :] + ws[1] * l_s[1, rows, :] + ws[2] * l_s[2, rows, :]
        o_ref[rows, :] = (num / den).astype(o_ref.dtype)
        return carry

    lax.fori_loop(0, nblk, merge, 0)


def _attention(proj, cos_t, sin_t, bsz, seq):
    for window, dil in ATTN_GROUPS:
        assert window // dil == ATTN_BLOCK and seq % (ATTN_BLOCK * dil) == 0
    pad_max = ATTN_BLOCK * max(d for _, d in ATTN_GROUPS)

    def head_spec(col0, g):
        blk0 = col0 // HEAD_DIM + g * ATTN_HPG
        return pl.BlockSpec((seq, HEAD_DIM), lambda b, h: (b, blk0 + h))

    in_specs = []
    for g in range(len(ATTN_GROUPS)):
        in_specs += [head_spec(COL_AQ, g), head_spec(COL_AK, g), head_spec(COL_AV, g)]
    in_specs += [pl.BlockSpec((seq, LANES), lambda b, h: (0, 0))] * 2
    return pl.pallas_call(
        _attn_kernel,
        grid=(bsz, ATTN_HPG),
        in_specs=in_specs,
        out_specs=pl.BlockSpec((seq, HEAD_DIM), lambda b, h: (b, h)),
        out_shape=jax.ShapeDtypeStruct((bsz * seq, ATTN_HPG * HEAD_DIM), BF16),
        scratch_shapes=[pltpu.VMEM((seq, LANES), F32),
                        pltpu.VMEM((seq + pad_max, LANES), F32),
                        pltpu.VMEM((seq + pad_max, LANES), F32),
                        pltpu.VMEM((3, seq, LANES), F32),
                        pltpu.VMEM((3, seq, LANES), F32),
                        pltpu.VMEM((3, seq, LANES), F32)],
        compiler_params=_params("parallel", "parallel"),
        name="attention",
    )(*([proj] * 9), cos_t, sin_t)


def _bf16_parts(a, n):
    parts, rest = [], a
    for _ in range(n):
        piece = rest.astype(BF16).astype(F32)
        parts.append(piece)
        rest = rest - piece
    return parts


def _lhs3(a):
    hi, lo = _bf16_parts(a, 2)
    return jnp.concatenate([hi, hi, lo], axis=1).astype(BF16)


def _rhs3(b):
    hi, lo = _bf16_parts(b, 2)
    return jnp.concatenate([hi, lo, hi], axis=0).astype(BF16)


def _split3_rows(b):
    return jnp.concatenate(_bf16_parts(b, 3), axis=0).astype(BF16)


def _deltanet_kernel(q_ref, k_ref, v_ref, z_ref, ba_ref, cwq_ref, cwk_ref, cwv_ref, alog_ref, dtb_ref, nw_ref,
                     o_ref, kq_s, b_s, o_s, cd_s):
    seq = q_ref.shape[0]
    nchunk = seq // DN_CHUNK
    c64 = DN_CHUNK
    ri = lax.broadcasted_iota(jnp.int32, (c64, c64), 0)
    ci = lax.broadcasted_iota(jnp.int32, (c64, c64), 1)
    lower_incl = ri >= ci
    lower_strict = ri > ci
    ones_lower = lower_incl.astype(F32)
    eye = (ri == ci).astype(F32)
    lane = lax.broadcasted_iota(jnp.int32, (c64, LANES), 1)
    lane16 = lax.broadcasted_iota(jnp.int32, (1, DN_HEADS), 1)

    eye_bf16 = eye.astype(BF16)
    ones3 = jnp.concatenate([ones_lower] * 3, axis=1).astype(BF16)

    def conv_silu(x_ref, cw_ref, hh, r0, c):
        cols = slice(hh * DN_DIM, (hh + 1) * DN_DIM)
        prev = x_ref[pl.ds(pl.multiple_of(jnp.maximum(r0 - SUBLANES, 0), SUBLANES), SUBLANES), cols]
        prev = jnp.where(c > 0, prev, 0.0)
        win = jnp.concatenate([prev, x_ref[pl.ds(r0, c64), cols]], axis=0)
        y = win[SUBLANES:, :] * cw_ref[DN_CONV - 1:DN_CONV, cols]
        for j in range(1, DN_CONV):
            y = y + pltpu.roll(win, j, 0)[SUBLANES:, :] * cw_ref[DN_CONV - 1 - j:DN_CONV - j, cols]
        return _silu(y)

    def prepare(c, hh):
        r0 = pl.multiple_of(c * c64, c64)
        head = pl.program_id(1) * DN_HB + hh
        qx = conv_silu(q_ref, cwq_ref, hh, r0, c)
        kx = conv_silu(k_ref, cwk_ref, hh, r0, c)
        vx = conv_silu(v_ref, cwv_ref, hh, r0, c)
        qn = qx * lax.rsqrt(jnp.sum(qx * qx, axis=-1, keepdims=True) + 1e-6) * (DN_DIM ** -0.5)
        kn = kx * lax.rsqrt(jnp.sum(kx * kx, axis=-1, keepdims=True) + 1e-6)
        ba = ba_ref[pl.ds(r0, c64), :]
        b_col = jnp.sum(jnp.where(lane == head, ba, 0.0), axis=-1, keepdims=True)
        a_col = jnp.sum(jnp.where(lane == head + DN_HEADS, ba, 0.0), axis=-1, keepdims=True)
        a_log = jnp.sum(jnp.where(lane16 == head, alog_ref[...], 0.0), axis=-1, keepdims=True)
        dt_b = jnp.sum(jnp.where(lane16 == head, dtb_ref[...], 0.0), axis=-1, keepdims=True)
        beta = jax.nn.sigmoid(b_col)
        ax = a_col + dt_b
        softplus = jnp.maximum(ax, 0.0) + jnp.log1p(jnp.exp(-jnp.abs(ax)))
        g = -jnp.exp(a_log) * softplus
        gm = jnp.where(lower_strict, jnp.broadcast_to(g, (c64, c64)), 0.0)
        kb = kn * beta
        return dict(qn=qn, kn=kn, vb=vx * beta, kb=kb, g=g, gm3=_split3_rows(gm),
                    kbqn=jnp.concatenate([kb, qn], axis=0).astype(BF16), kn16=kn.astype(BF16))

    def local(i, carry):
        todo = [(i * DN_CPI + j, hh) for j in range(DN_CPI) for hh in range(DN_HB)]
        ps = [prepare(c, hh) for c, hh in todo]
        diffs = [_dot(ones3, p["gm3"]) for p in ps]
        scores = [_dot_nt(p["kbqn"], p["kn16"]) for p in ps]
        mids = []
        for p, diff, sc in zip(ps, diffs, scores):
            gam = diff[:, 0:1] + p["g"][0:1, :]
            gam_last = gam[c64 - 1:c64, :]
            decay = jnp.where(lower_incl, jnp.exp(jnp.where(lower_incl, diff, 0.0)), 0.0)
            eg = jnp.exp(gam)
            mids.append(dict(nmat=-jnp.where(lower_strict, sc[:c64] * decay, 0.0), qk=sc[c64:] * decay,
                             rhs=jnp.concatenate([p["vb"], p["kb"] * eg], axis=-1), qd=p["qn"] * eg,
                             kd=(p["kn"] * jnp.exp(gam_last - gam)).astype(BF16),
                             cd=jnp.broadcast_to(jnp.exp(gam_last), (SUBLANES, LANES))))
        invs = [eye + m["nmat"] for m in mids]
        pws = [_dot(_lhs3(m["nmat"]), _rhs3(m["nmat"])) for m in mids]
        for _ in range(4):
            boths = [_dot(_lhs3(pw), _rhs3(jnp.concatenate([inv, pw], axis=1))) for inv, pw in zip(invs, pws)]
            invs = [inv + b[:, :c64] for inv, b in zip(invs, boths)]
            pws = [b[:, c64:] for b in boths]
        ys = [_dot(_lhs3(inv), _rhs3(m["rhs"])) for inv, m in zip(invs, mids)]
        sols = [y + _dot(_lhs3(pw), _rhs3(y)) for y, pw in zip(ys, pws)]
        kdts = [_dot_tn(m["kd"], eye_bf16) for m in mids]
        prods = [_dot(jnp.concatenate([kd_t, m["qk"]], axis=0).astype(BF16),
                      jnp.concatenate([sol[:, DN_DIM:], sol[:, :DN_DIM]], axis=1).astype(BF16))
                 for m, sol, kd_t in zip(mids, sols, kdts)]
        for (c, hh), m, pr in zip(todo, mids, prods):
            kq_s[hh, pl.ds(pl.multiple_of(c * (DN_DIM + c64), c64), DN_DIM + c64), :] = jnp.concatenate(
                [pr[:DN_DIM, :DN_DIM], m["qd"] - pr[DN_DIM:, :DN_DIM]], axis=0).astype(BF16)
            b_s[hh, pl.ds(pl.multiple_of(c * DN_DIM, DN_DIM), DN_DIM), :] = pr[:DN_DIM, DN_DIM:]
            o_s[hh, pl.ds(pl.multiple_of(c * c64, c64), c64), :] = pr[DN_DIM:, DN_DIM:]
            cd_s[hh, pl.ds(pl.multiple_of(c * SUBLANES, SUBLANES), SUBLANES), :] = m["cd"]
        return carry

    lax.fori_loop(0, nchunk // DN_CPI, local, 0)

    def scan(c, states):
        r0 = pl.multiple_of(c * c64, c64)
        rows = pl.ds(r0, c64)
        heads = range(DN_HB)
        rkq = pl.ds(pl.multiple_of(c * (DN_DIM + c64), c64), DN_DIM + c64)
        rb = pl.ds(pl.multiple_of(c * DN_DIM, DN_DIM), DN_DIM)
        cds = [cd_s[hh, pl.ds(pl.multiple_of(c * SUBLANES, SUBLANES), SUBLANES), :][0:1, :] for hh in heads]
        rs = [_dot(kq_s[hh, rkq, :], states[hh].astype(BF16)) for hh in heads]
        new_states = [states[hh] * cds[hh] - rs[hh][:DN_DIM] + b_s[hh, rb, :] for hh in heads]
        outs = []
        for hh in heads:
            o = rs[hh][DN_DIM:] + o_s[hh, rows, :]
            on = o * lax.rsqrt(jnp.mean(o * o, axis=-1, keepdims=True) + NORM_EPS) * nw_ref[...]
            outs.append((on * _silu(z_ref[rows, hh * DN_DIM:(hh + 1) * DN_DIM])).astype(o_ref.dtype))
        for hh in heads:
            o_ref[rows, hh * DN_DIM:(hh + 1) * DN_DIM] = outs[hh]
        return tuple(new_states)

    lax.fori_loop(0, nchunk, scan, tuple(jnp.zeros((DN_DIM, DN_DIM), F32) for _ in range(DN_HB)))


def _deltanet(proj, conv_w, a_log, dt_bias, norm_w, bsz, seq):
    wb = DN_DIM * DN_HB
    hblocks = DN_HEADS // DN_HB
    nchunk = seq // DN_CHUNK
    assert nchunk % DN_CPI == 0

    def act_spec(col0):
        blk0 = col0 // wb
        return pl.BlockSpec((seq, wb), lambda b, h: (b, blk0 + h))

    def cw_spec(part):
        blk0 = part * DN_WIDTH // wb
        return pl.BlockSpec((DN_CONV, wb), lambda b, h: (0, blk0 + h))

    return pl.pallas_call(
        _deltanet_kernel,
        grid=(bsz, hblocks),
        in_specs=[act_spec(COL_DQ), act_spec(COL_DK), act_spec(COL_DV), act_spec(COL_DZ),
                  pl.BlockSpec((seq, LANES), lambda b, h: (b, COL_BA // LANES)),
                  cw_spec(0), cw_spec(1), cw_spec(2),
                  pl.BlockSpec((1, DN_HEADS), lambda b, h: (0, 0)),
                  pl.BlockSpec((1, DN_HEADS), lambda b, h: (0, 0)),
                  pl.BlockSpec((1, DN_DIM), lambda b, h: (0, 0))],
        out_specs=pl.BlockSpec((seq, wb), lambda b, h: (b, h)),
        out_shape=jax.ShapeDtypeStruct((bsz * seq, DN_WIDTH), BF16),
        scratch_shapes=[pltpu.VMEM((DN_HB, nchunk * (DN_DIM + DN_CHUNK), DN_DIM), BF16),
                        pltpu.VMEM((DN_HB, nchunk * DN_DIM, DN_DIM), F32),
                        pltpu.VMEM((DN_HB, seq, DN_DIM), F32),
                        pltpu.VMEM((DN_HB, nchunk * SUBLANES, LANES), F32)],
        compiler_params=_params("parallel", "parallel"),
        name="deltanet",
    )(proj, proj, proj, proj, proj, conv_w, conv_w, conv_w,
      a_log.reshape(1, DN_HEADS), dt_bias.reshape(1, DN_HEADS), norm_w.reshape(1, DN_DIM))


def _mixout_kernel(ya_ref, og_ref, ga_ref, gd_ref, x_ref, ada_ref, wa_ref, wd_ref, wm_ref, npost_ref, npre_ref,
                   xmid_ref, h2_ref):
    y_attn = _dot(ya_ref[...], wa_ref[...])
    y_delta = _dot(og_ref[...], wd_ref[...])
    merged = jax.nn.sigmoid(ga_ref[...]) * y_attn + jax.nn.sigmoid(gd_ref[...]) * y_delta
    y = _dot(merged.astype(BF16), wm_ref[...])
    yn = y * lax.rsqrt(jnp.mean(y * y, axis=-1, keepdims=True) + NORM_EPS) * npost_ref[...]
    x_mid = x_ref[...] + ada_ref[2:3, :] * yn
    xmid_ref[...] = x_mid
    hn = x_mid * lax.rsqrt(jnp.mean(x_mid * x_mid, axis=-1, keepdims=True) + NORM_EPS) * npre_ref[...]
    h2_ref[...] = hn * (1.0 + ada_ref[4:5, :]) + ada_ref[3:4, :]


def _mixout(y_attn, o_gated, proj, x2, ada3, w_attn, w_delta, w_mix, norm_post, norm_pre_ffn, seq):
    t, d = x2.shape
    tm = 256
    const = dict(pipeline_mode=pl.Buffered(1))
    return pl.pallas_call(
        _mixout_kernel,
        grid=(t // tm,),
        in_specs=[pl.BlockSpec((tm, y_attn.shape[1]), lambda i: (i, 0)),
                  pl.BlockSpec((tm, d), lambda i: (i, 0)),
                  pl.BlockSpec((tm, d), lambda i: (i, COL_GA // d)),
                  pl.BlockSpec((tm, d), lambda i: (i, COL_GD // d)),
                  pl.BlockSpec((tm, d), lambda i: (i, 0)),
                  pl.BlockSpec((None, 6, d), lambda i: (i * tm // seq, 0, 0)),
                  pl.BlockSpec(w_attn.shape, lambda i: (0, 0), **const),
                  pl.BlockSpec(w_delta.shape, lambda i: (0, 0), **const),
                  pl.BlockSpec(w_mix.shape, lambda i: (0, 0), **const),
                  pl.BlockSpec((1, d), lambda i: (0, 0)),
                  pl.BlockSpec((1, d), lambda i: (0, 0))],
        out_specs=[pl.BlockSpec((tm, d), lambda i: (i, 0))] * 2,
        out_shape=[jax.ShapeDtypeStruct((t, d), F32)] * 2,
        compiler_params=_params("parallel"),
        name="mix_out",
    )(y_attn, o_gated, proj, proj, x2, ada3, w_attn, w_delta, w_mix, norm_post, norm_pre_ffn)


def _topk_rows(s, k):
    n = s.shape[0]
    rid = lax.broadcasted_iota(jnp.int32, s.shape, 0)
    vals, ids = [], []
    for _ in range(k):
        m = jnp.max(s, axis=0, keepdims=True)
        sel = jnp.min(jnp.where(s == m, rid, n), axis=0, keepdims=True)
        vals.append(m)
        ids.append(sel)
        s = jnp.where(rid == sel, -jnp.inf, s)
    return jnp.concatenate(vals, axis=0), jnp.concatenate(ids, axis=0)


def _peer_route_kernel(h_ref, wq_ref, keys_ref, idx_ref, gw_ref, q_s, idx_s, gw_s):
    tm = h_ref.shape[0]
    head = pl.program_id(1)
    half = PEER_QDIM // 2

    @pl.when(head == 0)
    def _():
        q = _dot(h_ref[...].astype(BF16), wq_ref[...])
        for hh in range(PEER_HEADS):
            q_s[hh] = q[:, hh * PEER_QDIM:(hh + 1) * PEER_QDIM]

    qh = q_s[head]
    tops = []
    for p in range(2):
        s = _dot_nt(keys_ref[0, p], qh[:, p * half:(p + 1) * half], HIGHEST)
        tops.append(_topk_rows(s, PEER_TOPK))
    (s0, i0), (s1, i1) = tops
    counts = [PEER_TOPK // (a + 1) for a in range(PEER_TOPK)]
    n_cand = sum(counts)
    n_pad = -n_cand % SUBLANES
    cand = jnp.concatenate([s0[a:a + 1, :] + s1[0:counts[a], :] for a in range(PEER_TOPK)]
                           + [jnp.full((n_pad, tm), -jnp.inf, F32)], axis=0)
    best_s, best_row = _topk_rows(cand, PEER_TOPK)
    a_sel = jnp.zeros_like(best_row)
    b_sel = best_row
    start = 0
    for a in range(PEER_TOPK - 1):
        start += counts[a]
        past = (best_row >= start).astype(jnp.int32)
        a_sel = a_sel + past
        b_sel = b_sel - past * counts[a]
    e0 = jnp.zeros_like(best_row)
    e1 = jnp.zeros_like(best_row)
    for a in range(PEER_TOPK):
        e0 = e0 + jnp.where(a_sel == a, i0[a:a + 1, :], 0)
        e1 = e1 + jnp.where(b_sel == a, i1[a:a + 1, :], 0)
    ex = jnp.exp(best_s - jnp.max(best_s, axis=0, keepdims=True))
    rows = pl.ds(pl.multiple_of(head * PEER_TOPK, PEER_TOPK), PEER_TOPK)
    idx_s[rows, :] = e0 * PEER_KEYS + e1
    gw_s[rows, :] = ex / jnp.sum(ex, axis=0, keepdims=True)

    @pl.when(head == PEER_HEADS - 1)
    def _():
        for bb in range(tm // PEER_TB):
            tok = slice(bb * PEER_TB, (bb + 1) * PEER_TB)
            gw_ref[bb] = gw_s[:, tok]
            idx_ref[bb] = idx_s[:, tok].T


def _peer_route(h2, w_query_bf16, keys):
    t, d = h2.shape
    tm = 256
    nb = tm // PEER_TB
    return pl.pallas_call(
        _peer_route_kernel,
        grid=(t // tm, PEER_HEADS),
        in_specs=[pl.BlockSpec((tm, d), lambda i, h: (i, 0)),
                  pl.BlockSpec(w_query_bf16.shape, lambda i, h: (0, 0)),
                  pl.BlockSpec((1, 2, PEER_KEYS, PEER_QDIM // 2), lambda i, h: (h, 0, 0, 0))],
        out_specs=[pl.BlockSpec((nb, PEER_TB, PEER_SLOTS), lambda i, h: (i, 0, 0)),
                   pl.BlockSpec((nb, PEER_SLOTS, PEER_TB), lambda i, h: (i, 0, 0))],
        out_shape=[jax.ShapeDtypeStruct((t // PEER_TB, PEER_TB, PEER_SLOTS), jnp.int32),
                   jax.ShapeDtypeStruct((t // PEER_TB, PEER_SLOTS, PEER_TB), F32)],
        scratch_shapes=[pltpu.VMEM((PEER_HEADS, tm, PEER_QDIM), F32),
                        pltpu.VMEM((PEER_SLOTS, tm), jnp.int32),
                        pltpu.VMEM((PEER_SLOTS, tm), F32)],
        compiler_params=_params("parallel", "arbitrary"),
        name="peer_route",
    )(h2, w_query_bf16, keys)


def _peer_gather_kernel(idx_hbm, gw_ref, h_ref, xmid_ref, ada_ref, nw_ref, down_hbm, up_hbm,
                        out_ref, buf, sem, idx_sm, idx_sem, ybuf):
    tb, d = h_ref.shape
    half = d // 2
    nsub = half // LANES
    tbl_rows = PEER_SLOTS * PEER_PITCH
    slot_rows = 2 * tbl_rows
    nbuf = PEER_NBUF
    ngroups = tb // nbuf
    step = pl.program_id(0)
    lane = lax.broadcasted_iota(jnp.int32, (PEER_SLOTS, tb), 1)
    hi_mask = jnp.uint32(0xFFFF0000)

    def fetch_idx(g, par):
        return pltpu.make_async_copy(idx_hbm.at[step, pl.ds(g * nbuf, nbuf)], idx_sm.at[par], idx_sem.at[par])

    def issue(par, j):
        for tab, tbl in enumerate((down_hbm, up_hbm)):
            for k in range(PEER_SLOTS):
                dst = buf.at[pl.ds(j * slot_rows + tab * tbl_rows + k * PEER_PITCH, nsub)]
                pltpu.make_async_copy(tbl.at[idx_sm[par, j, k]], dst, sem.at[j]).start(priority=k % 2)

    def wait(j):
        rows = pl.ds(0, 2 * PEER_SLOTS * nsub)
        pltpu.make_async_copy(buf.at[rows], buf.at[rows], sem.at[j]).wait()

    def tile(j, tab, s):
        w = buf[pl.ds(j * slot_rows + tab * tbl_rows + s, PEER_SLOTS, stride=PEER_PITCH), :]
        return pltpu.bitcast(w << 16, F32), pltpu.bitcast(w & hi_mask, F32)

    def compute(t, j):
        xrow = h_ref[pl.ds(t, 1), :]
        acc = None
        for s in range(nsub):
            lo, hi = tile(j, 0, s)
            term = lo * xrow[:, s * LANES:(s + 1) * LANES] + hi * xrow[:, half + s * LANES:half + (s + 1) * LANES]
            acc = term if acc is None else acc + term
        hcol = jnp.sum(acc, axis=-1, keepdims=True)
        gwcol = jnp.sum(jnp.where(lane == t, gw_ref[0], 0.0), axis=-1, keepdims=True)
        ccol = gwcol * _gelu_exact(hcol)
        los, his = [], []
        for s in range(nsub):
            lo, hi = tile(j, 1, s)
            los.append(jnp.sum(lo * ccol, axis=0, keepdims=True))
            his.append(jnp.sum(hi * ccol, axis=0, keepdims=True))
        ybuf[pl.ds(t, 1), :] = jnp.concatenate(los + his, axis=-1)

    def run_group(g, par_next, fetch_g):
        fetch_idx(g + 1, par_next).wait()
        if fetch_g is not None:
            fetch_idx(fetch_g, 1 - par_next).start()
        for j in range(nbuf):
            wait(j)
            compute(g * nbuf + j, j)
            issue(par_next, j)

    fetch_idx(0, 0).start()
    fetch_idx(0, 0).wait()
    fetch_idx(1, 1).start()
    for j in range(nbuf):
        issue(0, j)

    def pair(p, carry):
        run_group(2 * p, 1, 2 * p + 2)
        run_group(2 * p + 1, 0, 2 * p + 3)
        return carry

    lax.fori_loop(0, ngroups // 2 - 1, pair, 0)
    run_group(ngroups - 2, 1, None)
    for j in range(nbuf):
        wait(j)
        compute(tb - nbuf + j, j)
    y = ybuf[...]
    yn = y * lax.rsqrt(jnp.mean(y * y, axis=-1, keepdims=True) + NORM_EPS) * nw_ref[...]
    out_ref[...] = xmid_ref[...] + ada_ref[5:6, :] * yn


def _pack_rows_bf16(tbl):
    rows, d = tbl.shape
    half = d // 2
    lo = lax.bitcast_convert_type(tbl[:, :half].astype(BF16), jnp.uint16).astype(jnp.uint32)
    hi = lax.bitcast_convert_type(tbl[:, half:].astype(BF16), jnp.uint16).astype(jnp.uint32)
    return (lo | (hi << 16)).reshape(rows, half // LANES, LANES)


def _peer_gather(idx_t, gw_t, h2, x_mid, ada3, norm_w, down_packed, up_packed, seq):
    t, d = h2.shape
    blocks_per_batch = seq // PEER_TB
    nsub = d // 2 // LANES
    assert PEER_TB % (2 * PEER_NBUF) == 0 and nsub < PEER_PITCH
    return pl.pallas_call(
        _peer_gather_kernel,
        grid=(t // PEER_TB,),
        in_specs=[pl.BlockSpec(memory_space=pl.ANY),
                  pl.BlockSpec((1, PEER_SLOTS, PEER_TB), lambda i: (i, 0, 0)),
                  pl.BlockSpec((PEER_TB, d), lambda i: (i, 0)),
                  pl.BlockSpec((PEER_TB, d), lambda i: (i, 0)),
                  pl.BlockSpec((None, 6, d), lambda i: (i // blocks_per_batch, 0, 0)),
                  pl.BlockSpec((1, d), lambda i: (0, 0)),
                  pl.BlockSpec(memory_space=pl.ANY),
                  pl.BlockSpec(memory_space=pl.ANY)],
        out_specs=pl.BlockSpec((PEER_TB, d), lambda i: (i, 0)),
        out_shape=jax.ShapeDtypeStruct((t, d), F32),
        scratch_shapes=[pltpu.VMEM((PEER_NBUF * 2 * PEER_SLOTS * PEER_PITCH, LANES), jnp.uint32),
                        pltpu.SemaphoreType.DMA((PEER_NBUF,)),
                        pltpu.SMEM((2, PEER_NBUF, PEER_SLOTS), jnp.int32),
                        pltpu.SemaphoreType.DMA((2,)),
                        pltpu.VMEM((PEER_TB, d), F32)],
        compiler_params=_params("arbitrary"),
        name="peer_gather",
    )(idx_t, gw_t, h2, x_mid, ada3, norm_w, down_packed, up_packed)


def _permute_w_in(w_in):
    o_attn, o_dn, o_z = 0, 3 * ATTN_WIDTH, 3 * ATTN_WIDTH + 3 * DN_WIDTH
    o_b = o_z + DN_WIDTH
    o_gates = o_b + 2 * DN_HEADS
    parts = [w_in[:, o_gates:o_gates + 4096], w_in[:, o_dn:o_dn + 3 * DN_WIDTH], w_in[:, o_z:o_z + DN_WIDTH],
             w_in[:, o_attn:o_attn + 3 * ATTN_WIDTH], w_in[:, o_b:o_b + 2 * DN_HEADS]]
    w = jnp.concatenate(parts, axis=1)
    return jnp.pad(w, ((0, 0), (0, PROJ_WIDTH - w.shape[1]))).astype(BF16)


def _layer(x2, c, bsz, seq, w_ada, b_ada, norm_pre_mix, norm_post_mix, norm_pre_ffn, norm_post_ffn, w_in, conv_w,
           a_log, dt_bias, dn_norm_w, w_attn_out, w_delta_out, w_mix_out, peer_w_query, peer_sub_keys, peer_down,
           peer_up, cos_t, sin_t):
    d = x2.shape[1]
    row = lambda v: v.reshape(1, -1)
    ada3 = _ada(c, w_ada, b_ada).reshape(bsz, 6, d)
    proj = _inproj(x2, ada3, row(norm_pre_mix), _permute_w_in(w_in), seq)
    y_attn = _attention(proj, cos_t, sin_t, bsz, seq)
    o_gated = _deltanet(proj, conv_w, a_log, dt_bias, dn_norm_w, bsz, seq)
    x_mid, h2 = _mixout(y_attn, o_gated, proj, x2, ada3, w_attn_out.astype(BF16), w_delta_out.astype(BF16),
                        w_mix_out.astype(BF16), row(norm_post_mix), row(norm_pre_ffn), seq)
    idx_t, gw_t = _peer_route(h2, peer_w_query.astype(BF16), peer_sub_keys)
    return _peer_gather(idx_t, gw_t, h2, x_mid, ada3, row(norm_post_ffn), _pack_rows_bf16(peer_down),
                        _pack_rows_bf16(peer_up), seq)


def kernel(x, c, w_ada, b_ada, norm_pre_mix, norm_post_mix, norm_pre_ffn, norm_post_ffn, w_in, conv_w, a_log, dt_bias, dn_norm_w, w_attn_out, w_delta_out, w_mix_out, peer_w_query, peer_sub_keys, peer_down, peer_up):
    bsz, seq, d = x.shape
    x2 = x.reshape(bsz * seq, d)
    cos_t, sin_t = _rope_tables(seq)
    for layer in range(w_ada.shape[0]):
        x2 = _layer(x2, c, bsz, seq, w_ada[layer], b_ada[layer], norm_pre_mix[layer], norm_post_mix[layer],
                    norm_pre_ffn[layer], norm_post_ffn[layer], w_in[layer], conv_w[layer], a_log[layer],
                    dt_bias[layer], dn_norm_w[layer], w_attn_out[layer], w_delta_out[layer], w_mix_out[layer],
                    peer_w_query[layer], peer_sub_keys[layer], peer_down[layer], peer_up[layer], cos_t, sin_t)
    return x2.reshape(bsz, seq, d)
```

```python
import functools
import math

import jax
import jax.numpy as jnp
from jax import lax
from jax.experimental import pallas as pl
from jax.experimental.pallas import tpu as pltpu

F32 = jnp.float32
BF16 = jnp.bfloat16
HIGHEST = lax.Precision.HIGHEST
LANES = 128
SUBLANES = 8
VMEM_LIMIT = 56 * 1024 * 1024

NORM_EPS = 1e-6
HEAD_DIM = 128
ATTN_GROUPS = ((128, 1), (512, 4), (2048, 16))
ATTN_HPG = 4
ATTN_HEADS = ATTN_HPG * len(ATTN_GROUPS)
ATTN_WIDTH = ATTN_HEADS * HEAD_DIM
ATTN_BLOCK = 128
ATTN_BPI = 4
ROPE_THETA = 500000.0
ROPE_DIM = HEAD_DIM // 4
DN_HEADS = 16
DN_DIM = 128
DN_WIDTH = DN_HEADS * DN_DIM
DN_CONV = 4
DN_CHUNK = 64
DN_HB = 2
DN_CPI = 4
PEER_HEADS = 8
PEER_KEYS = 128
PEER_QDIM = 128
PEER_TOPK = 16
PEER_SLOTS = PEER_HEADS * PEER_TOPK
PEER_TB = 128
PEER_NBUF = 8
PEER_PITCH = 17

COL_GA = 0
COL_GD = COL_GA + 2048
COL_DQ = COL_GD + 2048
COL_DK = COL_DQ + DN_WIDTH
COL_DV = COL_DK + DN_WIDTH
COL_DZ = COL_DV + DN_WIDTH
COL_AQ = COL_DZ + DN_WIDTH
COL_AK = COL_AQ + ATTN_WIDTH
COL_AV = COL_AK + ATTN_WIDTH
COL_BA = COL_AV + ATTN_WIDTH
PROJ_WIDTH = 17 * 1024


def _silu(x):
    return x * jax.nn.sigmoid(x)


def _gelu_exact(x):
    return 0.5 * x * (1.0 + lax.erf(x * (2.0 ** -0.5)))


def _dot(a, b, precision=None):
    return jnp.dot(a, b, preferred_element_type=F32, precision=precision)


def _dot_nt(a, b, precision=None):
    return lax.dot_general(a, b, (((1,), (1,)), ((), ())), preferred_element_type=F32, precision=precision)


def _dot_tn(a, b, precision=None):
    return lax.dot_general(a, b, (((0,), (0,)), ((), ())), preferred_element_type=F32, precision=precision)


def _params(*sem):
    return pltpu.CompilerParams(dimension_semantics=sem, vmem_limit_bytes=VMEM_LIMIT)


def _ada_kernel(c_ref, w_ref, b_ref, o_ref):
    o_ref[...] = _dot(_silu(c_ref[...]), w_ref[...], HIGHEST) + b_ref[...]


def _ada(c, w, b):
    bsz, d = c.shape
    n = w.shape[1]
    tn = 1536
    return pl.pallas_call(
        _ada_kernel,
        grid=(n // tn,),
        in_specs=[pl.BlockSpec((bsz, d), lambda j: (0, 0)),
                  pl.BlockSpec((d, tn), lambda j: (0, j)),
                  pl.BlockSpec((1, tn), lambda j: (0, j))],
        out_specs=pl.BlockSpec((bsz, tn), lambda j: (0, j)),
        out_shape=jax.ShapeDtypeStruct((bsz, n), F32),
        compiler_params=_params("parallel"),
        name="ada",
    )(c, w, b.reshape(1, n))


def _rope_kernel(cos_ref, sin_ref):
    rows = cos_ref.shape[0]
    pos = (lax.broadcasted_iota(jnp.int32, (rows, LANES), 0) + pl.program_id(0) * rows).astype(F32)
    lane = lax.broadcasted_iota(jnp.int32, (rows, LANES), 1)
    half = ROPE_DIM // 2
    inv_freq = jnp.exp((lane % half).astype(F32) * (-(2.0 / ROPE_DIM) * math.log(ROPE_THETA)))
    ang = pos * inv_freq
    cos_ref[...] = jnp.where(lane < ROPE_DIM, jnp.cos(ang), 1.0)
    sin = jnp.sin(ang)
    sin_ref[...] = jnp.where(lane < half, -sin, jnp.where(lane < ROPE_DIM, sin, 0.0))


def _rope_tables(seq):
    rows = 256
    return pl.pallas_call(
        _rope_kernel,
        grid=(seq // rows,),
        out_specs=[pl.BlockSpec((rows, LANES), lambda i: (i, 0))] * 2,
        out_shape=[jax.ShapeDtypeStruct((seq, LANES), F32)] * 2,
        compiler_params=_params("parallel"),
        name="rope",
    )()


def _inproj_kernel(x_ref, ada_ref, nw_ref, w_ref, o_ref, h_ref):
    @pl.when(pl.program_id(1) == 0)
    def _():
        x = x_ref[...]
        y = x * lax.rsqrt(jnp.mean(x * x, axis=-1, keepdims=True) + NORM_EPS) * nw_ref[...]
        h_ref[...] = (y * (1.0 + ada_ref[1:2, :]) + ada_ref[0:1, :]).astype(BF16)

    o_ref[...] = _dot(h_ref[...], w_ref[...])


def _inproj(x2, ada3, norm_w, w_bf16, seq):
    t, d = x2.shape
    n = w_bf16.shape[1]
    tm, tn = 1024, 1024
    return pl.pallas_call(
        _inproj_kernel,
        grid=(t // tm, n // tn),
        in_specs=[pl.BlockSpec((tm, d), lambda i, j: (i, 0)),
                  pl.BlockSpec((None, 6, d), lambda i, j: (i * tm // seq, 0, 0)),
                  pl.BlockSpec((1, d), lambda i, j: (0, 0)),
                  pl.BlockSpec((d, tn), lambda i, j: (0, j))],
        out_specs=pl.BlockSpec((tm, tn), lambda i, j: (i, j)),
        out_shape=jax.ShapeDtypeStruct((t, n), F32),
        scratch_shapes=[pltpu.VMEM((tm, d), BF16)],
        compiler_params=_params("parallel", "arbitrary"),
        name="in_proj",
    )(x2, ada3, norm_w, w_bf16)


def _attn_kernel(q1, k1, v1, q2, k2, v2, q3, k3, v3, cos_ref, sin_ref, o_ref, qs, ks, vs, acc_s, m_s, l_s):
    seq = q1.shape[0]
    nblk = seq // ATTN_BLOCK
    half = ROPE_DIM // 2
    pr = lax.broadcasted_iota(jnp.int32, (LANES, LANES), 0)
    pc = lax.broadcasted_iota(jnp.int32, (LANES, LANES), 1)
    perm = (((pr == pc + half) & (pc < half)) | ((pr == pc - half) & (pc >= half) & (pc < ROPE_DIM))).astype(BF16)
    perm2 = jnp.concatenate([perm, perm], axis=0)

    def rotary(t):
        hi, lo = _bf16_parts(t, 2)
        partner = _dot(jnp.concatenate([hi, lo], axis=1).astype(BF16), perm2)
        return t * cos_ref[...] + partner * sin_ref[...]

    qi = lax.broadcasted_iota(jnp.int32, (ATTN_BLOCK, 2 * ATTN_BLOCK), 0)
    kj = lax.broadcasted_iota(jnp.int32, (ATTN_BLOCK, 2 * ATTN_BLOCK), 1)
    dist = qi + ATTN_BLOCK - kj
    scale = HEAD_DIM ** -0.5

    for g, (refs, (window, dil)) in enumerate(zip(((q1, k1, v1), (q2, k2, v2), (q3, k3, v3)), ATTN_GROUPS)):
        q_ref, k_ref, v_ref = refs
        n_back = window // dil
        pad = ATTN_BLOCK * dil
        in_window = (dist >= 0) & (dist <= n_back)
        qs[...] = rotary(q_ref[...])
        ks[0:pad, :] = jnp.zeros((pad, LANES), F32)
        vs[0:pad, :] = jnp.zeros((pad, LANES), F32)
        ks[pad:pad + seq, :] = rotary(k_ref[...])
        vs[pad:pad + seq, :] = v_ref[...]

        def rows_of(idx, dil=dil, pad=pad):
            res = lax.rem(idx, dil)
            nb = idx // dil
            start = res + nb * pad
            if dil == 1:
                return nb, pl.ds(start, ATTN_BLOCK), pl.ds(start, 2 * ATTN_BLOCK)
            return nb, pl.ds(start, ATTN_BLOCK, stride=dil), pl.ds(start, 2 * ATTN_BLOCK, stride=dil)

        def blocks(i, carry, g=g, rows_of=rows_of, in_window=in_window):
            rows = [rows_of(i * ATTN_BPI + j) for j in range(ATTN_BPI)]
            ss = [_dot_nt(qs[rq, :], ks[rk, :]) * scale for _, rq, rk in rows]
            ps, ms = [], []
            for (nb, _, _), s in zip(rows, ss):
                s = jnp.where(in_window & ((nb > 0) | (kj >= ATTN_BLOCK)), s, -jnp.inf)
                m = jnp.max(s, axis=-1, keepdims=True)
                ms.append(m)
                ps.append(jnp.exp(s - m))
            accs = [_dot(p, vs[rk, :]) for (_, _, rk), p in zip(rows, ps)]
            for (_, rq, _), acc, m, p in zip(rows, accs, ms, ps):
                acc_s[g, rq, :] = acc
                m_s[g, rq, :] = jnp.broadcast_to(m, (ATTN_BLOCK, LANES))
                l_s[g, rq, :] = jnp.broadcast_to(jnp.sum(p, axis=-1, keepdims=True), (ATTN_BLOCK, LANES))
            return carry

        lax.fori_loop(0, nblk // ATTN_BPI, blocks, 0)

    def merge(i, carry):
        rows = pl.ds(pl.multiple_of(i * ATTN_BLOCK, ATTN_BLOCK), ATTN_BLOCK)
        ms = [m_s[g, rows, :] for g in range(3)]
        mx = jnp.maximum(jnp.maximum(ms[0], ms[1]), ms[2])
        ws = [jnp.exp(m - mx) for m in ms]
        num = ws[0] * acc_s[0, rows, :] + ws[1] * acc_s[1, rows, :] + ws[2] * acc_s[2, rows, :]
        den = ws[0] * l_s[0, rows, :] + ws[1] * l_s[1, rows, :] + ws[2] * l_s[2, rows, :]
        o_ref[rows, :] = (num / den).astype(o_ref.dtype)
        return carry

    lax.fori_loop(0, nblk, merge, 0)


def _attention(proj, cos_t, sin_t, bsz, seq):
    for window, dil in ATTN_GROUPS:
        assert window // dil == ATTN_BLOCK and seq % (ATTN_BLOCK * dil) == 0
    pad_max = ATTN_BLOCK * max(d for _, d in ATTN_GROUPS)

    def head_spec(col0, g):
        blk0 = col0 // HEAD_DIM + g * ATTN_HPG
        return pl.BlockSpec((seq, HEAD_DIM), lambda b, h: (b, blk0 + h))

    in_specs = []
    for g in range(len(ATTN_GROUPS)):
        in_specs += [head_spec(COL_AQ, g), head_spec(COL_AK, g), head_spec(COL_AV, g)]
    in_specs += [pl.BlockSpec((seq, LANES), lambda b, h: (0, 0))] * 2
    return pl.pallas_call(
        _attn_kernel,
        grid=(bsz, ATTN_HPG),
        in_specs=in_specs,
        out_specs=pl.BlockSpec((seq, HEAD_DIM), lambda b, h: (b, h)),
        out_shape=jax.ShapeDtypeStruct((bsz * seq, ATTN_HPG * HEAD_DIM), BF16),
        scratch_shapes=[pltpu.VMEM((seq, LANES), F32),
                        pltpu.VMEM((seq + pad_max, LANES), F32),
                        pltpu.VMEM((seq + pad_max, LANES), F32),
                        pltpu.VMEM((3, seq, LANES), F32),
                        pltpu.VMEM((3, seq, LANES), F32),
                        pltpu.VMEM((3, seq, LANES), F32)],
        compiler_params=_params("parallel", "parallel"),
        name="attention",
    )(*([proj] * 9), cos_t, sin_t)


def _bf16_parts(a, n):
    parts, rest = [], a
    for _ in range(n):
        piece = rest.astype(BF16).astype(F32)
        parts.append(piece)
        rest = rest - piece
    return parts


def _lhs3(a):
    hi, lo = _bf16_parts(a, 2)
    return jnp.concatenate([hi, hi, lo], axis=1).astype(BF16)


def _rhs3(b):
    hi, lo = _bf16_parts(b, 2)
    return jnp.concatenate([hi, lo, hi], axis=0).astype(BF16)


def _split3_rows(b):
    return jnp.concatenate(_bf16_parts(b, 3), axis=0).astype(BF16)


def _deltanet_kernel(q_ref, k_ref, v_ref, z_ref, ba_ref, cwq_ref, cwk_ref, cwv_ref, alog_ref, dtb_ref, nw_ref,
                     o_ref, kq_s, b_s, o_s, cd_s):
    seq = q_ref.shape[0]
    nchunk = seq // DN_CHUNK
    c64 = DN_CHUNK
    ri = lax.broadcasted_iota(jnp.int32, (c64, c64), 0)
    ci = lax.broadcasted_iota(jnp.int32, (c64, c64), 1)
    lower_incl = ri >= ci
    lower_strict = ri > ci
    ones_lower = lower_incl.astype(F32)
    eye = (ri == ci).astype(F32)
    lane = lax.broadcasted_iota(jnp.int32, (c64, LANES), 1)
    lane16 = lax.broadcasted_iota(jnp.int32, (1, DN_HEADS), 1)

    eye_bf16 = eye.astype(BF16)
    ones3 = jnp.concatenate([ones_lower] * 3, axis=1).astype(BF16)

    def conv_silu(x_ref, cw_ref, hh, r0, c):
        cols = slice(hh * DN_DIM, (hh + 1) * DN_DIM)
        prev = x_ref[pl.ds(pl.multiple_of(jnp.maximum(r0 - SUBLANES, 0), SUBLANES), SUBLANES), cols]
        prev = jnp.where(c > 0, prev, 0.0)
        win = jnp.concatenate([prev, x_ref[pl.ds(r0, c64), cols]], axis=0)
        y = win[SUBLANES:, :] * cw_ref[DN_CONV - 1:DN_CONV, cols]
        for j in range(1, DN_CONV):
            y = y + pltpu.roll(win, j, 0)[SUBLANES:, :] * cw_ref[DN_CONV - 1 - j:DN_CONV - j, cols]
        return _silu(y)

    def prepare(c, hh):
        r0 = pl.multiple_of(c * c64, c64)
        head = pl.program_id(1) * DN_HB + hh
        qx = conv_silu(q_ref, cwq_ref, hh, r0, c)
        kx = conv_silu(k_ref, cwk_ref, hh, r0, c)
        vx = conv_silu(v_ref, cwv_ref, hh, r0, c)
        qn = qx * lax.rsqrt(jnp.sum(qx * qx, axis=-1, keepdims=True) + 1e-6) * (DN_DIM ** -0.5)
        kn = kx * lax.rsqrt(jnp.sum(kx * kx, axis=-1, keepdims=True) + 1e-6)
        ba = ba_ref[pl.ds(r0, c64), :]
        b_col = jnp.sum(jnp.where(lane == head, ba, 0.0), axis=-1, keepdims=True)
        a_col = jnp.sum(jnp.where(lane == head + DN_HEADS, ba, 0.0), axis=-1, keepdims=True)
        a_log = jnp.sum(jnp.where(lane16 == head, alog_ref[...], 0.0), axis=-1, keepdims=True)
        dt_b = jnp.sum(jnp.where(lane16 == head, dtb_ref[...], 0.0), axis=-1, keepdims=True)
        beta = jax.nn.sigmoid(b_col)
        ax = a_col + dt_b
        softplus = jnp.maximum(ax, 0.0) + jnp.log1p(jnp.exp(-jnp.abs(ax)))
        g = -jnp.exp(a_log) * softplus
        gm = jnp.where(lower_strict, jnp.broadcast_to(g, (c64, c64)), 0.0)
        kb = kn * beta
        return dict(qn=qn, kn=kn, vb=vx * beta, kb=kb, g=g, gm3=_split3_rows(gm),
                    kbqn=jnp.concatenate([kb, qn], axis=0).astype(BF16), kn16=kn.astype(BF16))

    def local(i, carry):
        todo = [(i * DN_CPI + j, hh) for j in range(DN_CPI) for hh in range(DN_HB)]
        ps = [prepare(c, hh) for c, hh in todo]
        diffs = [_dot(ones3, p["gm3"]) for p in ps]
        scores = [_dot_nt(p["kbqn"], p["kn16"]) for p in ps]
        mids = []
        for p, diff, sc in zip(ps, diffs, scores):
            gam = diff[:, 0:1] + p["g"][0:1, :]
            gam_last = gam[c64 - 1:c64, :]
            decay = jnp.where(lower_incl, jnp.exp(jnp.where(lower_incl, diff, 0.0)), 0.0)
            eg = jnp.exp(gam)
            mids.append(dict(nmat=-jnp.where(lower_strict, sc[:c64] * decay, 0.0), qk=sc[c64:] * decay,
                             rhs=jnp.concatenate([p["vb"], p["kb"] * eg], axis=-1), qd=p["qn"] * eg,
                             kd=(p["kn"] * jnp.exp(gam_last - gam)).astype(BF16),
                             cd=jnp.broadcast_to(jnp.exp(gam_last), (SUBLANES, LANES))))
        invs = [eye + m["nmat"] for m in mids]
        pws = [_dot(_lhs3(m["nmat"]), _rhs3(m["nmat"])) for m in mids]
        for _ in range(4):
            boths = [_dot(_lhs3(pw), _rhs3(jnp.concatenate([inv, pw], axis=1))) for inv, pw in zip(invs, pws)]
            invs = [inv + b[:, :c64] for inv, b in zip(invs, boths)]
            pws = [b[:, c64:] for b in boths]
        ys = [_dot(_lhs3(inv), _rhs3(m["rhs"])) for inv, m in zip(invs, mids)]
        sols = [y + _dot(_lhs3(pw), _rhs3(y)) for y, pw in zip(ys, pws)]
        kdts = [_dot_tn(m["kd"], eye_bf16) for m in mids]
        prods = [_dot(jnp.concatenate([kd_t, m["qk"]], axis=0).astype(BF16),
                      jnp.concatenate([sol[:, DN_DIM:], sol[:, :DN_DIM]], axis=1).astype(BF16))
                 for m, sol, kd_t in zip(mids, sols, kdts)]
        for (c, hh), m, pr in zip(todo, mids, prods):
            kq_s[hh, pl.ds(pl.multiple_of(c * (DN_DIM + c64), c64), DN_DIM + c64), :] = jnp.concatenate(
                [pr[:DN_DIM, :DN_DIM], m["qd"] - pr[DN_DIM:, :DN_DIM]], axis=0).astype(BF16)
            b_s[hh, pl.ds(pl.multiple_of(c * DN_DIM, DN_DIM), DN_DIM), :] = pr[:DN_DIM, DN_DIM:]
            o_s[hh, pl.ds(pl.multiple_of(c * c64, c64), c64), :] = pr[DN_DIM:, DN_DIM:]
            cd_s[hh, pl.ds(pl.multiple_of(c * SUBLANES, SUBLANES), SUBLANES), :] = m["cd"]
        return carry

    lax.fori_loop(0, nchunk // DN_CPI, local, 0)

    def scan(c, states):
        r0 = pl.multiple_of(c * c64, c64)
        rows = pl.ds(r0, c64)
        heads = range(DN_HB)
        rkq = pl.ds(pl.multiple_of(c * (DN_DIM + c64), c64), DN_DIM + c64)
        rb = pl.ds(pl.multiple_of(c * DN_DIM, DN_DIM), DN_DIM)
        cds = [cd_s[hh, pl.ds(pl.multiple_of(c * SUBLANES, SUBLANES), SUBLANES), :][0:1, :] for hh in heads]
        rs = [_dot(kq_s[hh, rkq, :], states[hh].astype(BF16)) for hh in heads]
        new_states = [states[hh] * cds[hh] - rs[hh][:DN_DIM] + b_s[hh, rb, :] for hh in heads]
        outs = []
        for hh in heads:
            o = rs[hh][DN_DIM:] + o_s[hh, rows, :]
            on = o * lax.rsqrt(jnp.mean(o * o, axis=-1, keepdims=True) + NORM_EPS) * nw_ref[...]
            outs.append((on * _silu(z_ref[rows, hh * DN_DIM:(hh + 1) * DN_DIM])).astype(o_ref.dtype))
        for hh in heads:
            o_ref[rows, hh * DN_DIM:(hh + 1) * DN_DIM] = outs[hh]
        return tuple(new_states)

    lax.fori_loop(0, nchunk, scan, tuple(jnp.zeros((DN_DIM, DN_DIM), F32) for _ in range(DN_HB)))


def _deltanet(proj, conv_w, a_log, dt_bias, norm_w, bsz, seq):
    wb = DN_DIM * DN_HB
    hblocks = DN_HEADS // DN_HB
    nchunk = seq // DN_CHUNK
    assert nchunk % DN_CPI == 0

    def act_spec(col0):
        blk0 = col0 // wb
        return pl.BlockSpec((seq, wb), lambda b, h: (b, blk0 + h))

    def cw_spec(part):
        blk0 = part * DN_WIDTH // wb
        return pl.BlockSpec((DN_CONV, wb), lambda b, h: (0, blk0 + h))

    return pl.pallas_call(
        _deltanet_kernel,
        grid=(bsz, hblocks),
        in_specs=[act_spec(COL_DQ), act_spec(COL_DK), act_spec(COL_DV), act_spec(COL_DZ),
                  pl.BlockSpec((seq, LANES), lambda b, h: (b, COL_BA // LANES)),
                  cw_spec(0), cw_spec(1), cw_spec(2),
                  pl.BlockSpec((1, DN_HEADS), lambda b, h: (0, 0)),
                  pl.BlockSpec((1, DN_HEADS), lambda b, h: (0, 0)),
                  pl.BlockSpec((1, DN_DIM), lambda b, h: (0, 0))],
        out_specs=pl.BlockSpec((seq, wb), lambda b, h: (b, h)),
        out_shape=jax.ShapeDtypeStruct((bsz * seq, DN_WIDTH), BF16),
        scratch_shapes=[pltpu.VMEM((DN_HB, nchunk * (DN_DIM + DN_CHUNK), DN_DIM), BF16),
                        pltpu.VMEM((DN_HB, nchunk * DN_DIM, DN_DIM), F32),
                        pltpu.VMEM((DN_HB, seq, DN_DIM), F32),
                        pltpu.VMEM((DN_HB, nchunk * SUBLANES, LANES), F32)],
        compiler_params=_params("parallel", "parallel"),
        name="deltanet",
    )(proj, proj, proj, proj, proj, conv_w, conv_w, conv_w,
      a_log.reshape(1, DN_HEADS), dt_bias.reshape(1, DN_HEADS), norm_w.reshape(1, DN_DIM))


def _mixout_kernel(ya_ref, og_ref, ga_ref, gd_ref, x_ref, ada_ref, wa_ref, wd_ref, wm_ref, npost_ref, npre_ref,
                   xmid_ref, h2_ref):
    y_attn = _dot(ya_ref[...], wa_ref[...])
    y_delta = _dot(og_ref[...], wd_ref[...])
    merged = jax.nn.sigmoid(ga_ref[...]) * y_attn + jax.nn.sigmoid(gd_ref[...]) * y_delta
    y = _dot(merged.astype(BF16), wm_ref[...])
    yn = y * lax.rsqrt(jnp.mean(y * y, axis=-1, keepdims=True) + NORM_EPS) * npost_ref[...]
    x_mid = x_ref[...] + ada_ref[2:3, :] * yn
    xmid_ref[...] = x_mid
    hn = x_mid * lax.rsqrt(jnp.mean(x_mid * x_mid, axis=-1, keepdims=True) + NORM_EPS) * npre_ref[...]
    h2_ref[...] = hn * (1.0 + ada_ref[4:5, :]) + ada_ref[3:4, :]


def _mixout(y_attn, o_gated, proj, x2, ada3, w_attn, w_delta, w_mix, norm_post, norm_pre_ffn, seq):
    t, d = x2.shape
    tm = 256
    const = dict(pipeline_mode=pl.Buffered(1))
    return pl.pallas_call(
        _mixout_kernel,
        grid=(t // tm,),
        in_specs=[pl.BlockSpec((tm, y_attn.shape[1]), lambda i: (i, 0)),
                  pl.BlockSpec((tm, d), lambda i: (i, 0)),
                  pl.BlockSpec((tm, d), lambda i: (i, COL_GA // d)),
                  pl.BlockSpec((tm, d), lambda i: (i, COL_GD // d)),
                  pl.BlockSpec((tm, d), lambda i: (i, 0)),
                  pl.BlockSpec((None, 6, d), lambda i: (i * tm // seq, 0, 0)),
                  pl.BlockSpec(w_attn.shape, lambda i: (0, 0), **const),
                  pl.BlockSpec(w_delta.shape, lambda i: (0, 0), **const),
                  pl.BlockSpec(w_mix.shape, lambda i: (0, 0), **const),
                  pl.BlockSpec((1, d), lambda i: (0, 0)),
                  pl.BlockSpec((1, d), lambda i: (0, 0))],
        out_specs=[pl.BlockSpec((tm, d), lambda i: (i, 0))] * 2,
        out_shape=[jax.ShapeDtypeStruct((t, d), F32)] * 2,
        compiler_params=_params("parallel"),
        name="mix_out",
    )(y_attn, o_gated, proj, proj, x2, ada3, w_attn, w_delta, w_mix, norm_post, norm_pre_ffn)


def _topk_rows(s, k):
    n = s.shape[0]
    rid = lax.broadcasted_iota(jnp.int32, s.shape, 0)
    vals, ids = [], []
    for _ in range(k):
        m = jnp.max(s, axis=0, keepdims=True)
        sel = jnp.min(jnp.where(s == m, rid, n), axis=0, keepdims=True)
        vals.append(m)
        ids.append(sel)
        s = jnp.where(rid == sel, -jnp.inf, s)
    return jnp.concatenate(vals, axis=0), jnp.concatenate(ids, axis=0)


def _peer_route_kernel(h_ref, wq_ref, keys_ref, idx_ref, gw_ref, q_s, idx_s, gw_s):
    tm = h_ref.shape[0]
    head = pl.program_id(1)
    half = PEER_QDIM // 2

    @pl.when(head == 0)
    def _():
        q = _dot(h_ref[...].astype(BF16), wq_ref[...])
        for hh in range(PEER_HEADS):
            q_s[hh] = q[:, hh * PEER_QDIM:(hh + 1) * PEER_QDIM]

    qh = q_s[head]
    tops = []
    for p in range(2):
        s = _dot_nt(keys_ref[0, p], qh[:, p * half:(p + 1) * half], HIGHEST)
        tops.append(_topk_rows(s, PEER_TOPK))
    (s0, i0), (s1, i1) = tops
    counts = [PEER_TOPK // (a + 1) for a in range(PEER_TOPK)]
    n_cand = sum(counts)
    n_pad = -n_cand % SUBLANES
    cand = jnp.concatenate([s0[a:a + 1, :] + s1[0:counts[a], :] for a in range(PEER_TOPK)]
                           + [jnp.full((n_pad, tm), -jnp.inf, F32)], axis=0)
    best_s, best_row = _topk_rows(cand, PEER_TOPK)
    a_sel = jnp.zeros_like(best_row)
    b_sel = best_row
    start = 0
    for a in range(PEER_TOPK - 1):
        start += counts[a]
        past = (best_row >= start).astype(jnp.int32)
        a_sel = a_sel + past
        b_sel = b_sel - past * counts[a]
    e0 = jnp.zeros_like(best_row)
    e1 = jnp.zeros_like(best_row)
    for a in range(PEER_TOPK):
        e0 = e0 + jnp.where(a_sel == a, i0[a:a + 1, :], 0)
        e1 = e1 + jnp.where(b_sel == a, i1[a:a + 1, :], 0)
    ex = jnp.exp(best_s - jnp.max(best_s, axis=0, keepdims=True))
    rows = pl.ds(pl.multiple_of(head * PEER_TOPK, PEER_TOPK), PEER_TOPK)
    idx_s[rows, :] = e0 * PEER_KEYS + e1
    gw_s[rows, :] = ex / jnp.sum(ex, axis=0, keepdims=True)

    @pl.when(head == PEER_HEADS - 1)
    def _():
        for bb in range(tm // PEER_TB):
            tok = slice(bb * PEER_TB, (bb + 1) * PEER_TB)
            gw_ref[bb] = gw_s[:, tok]
            idx_ref[bb] = idx_s[:, tok].T


def _peer_route(h2, w_query_bf16, keys):
    t, d = h2.shape
    tm = 256
    nb = tm // PEER_TB
    return pl.pallas_call(
        _peer_route_kernel,
        grid=(t // tm, PEER_HEADS),
        in_specs=[pl.BlockSpec((tm, d), lambda i, h: (i, 0)),
                  pl.BlockSpec(w_query_bf16.shape, lambda i, h: (0, 0)),
                  pl.BlockSpec((1, 2, PEER_KEYS, PEER_QDIM // 2), lambda i, h: (h, 0, 0, 0))],
        out_specs=[pl.BlockSpec((nb, PEER_TB, PEER_SLOTS), lambda i, h: (i, 0, 0)),
                   pl.BlockSpec((nb, PEER_SLOTS, PEER_TB), lambda i, h: (i, 0, 0))],
        out_shape=[jax.ShapeDtypeStruct((t // PEER_TB, PEER_TB, PEER_SLOTS), jnp.int32),
                   jax.ShapeDtypeStruct((t // PEER_TB, PEER_SLOTS, PEER_TB), F32)],
        scratch_shapes=[pltpu.VMEM((PEER_HEADS, tm, PEER_QDIM), F32),
                        pltpu.VMEM((PEER_SLOTS, tm), jnp.int32),
                        pltpu.VMEM((PEER_SLOTS, tm), F32)],
        compiler_params=_params("parallel", "arbitrary"),
        name="peer_route",
    )(h2, w_query_bf16, keys)


def _peer_gather_kernel(idx_hbm, gw_ref, h_ref, xmid_ref, ada_ref, nw_ref, tbl_hbm,
                        out_ref, buf, sem, idx_sm, idx_sem, ybuf):
    tb, d = h_ref.shape
    half = d // 2
    nsub = half // LANES
    slot_rows = PEER_SLOTS * PEER_PITCH
    nbuf = PEER_NBUF
    ngroups = tb // nbuf
    step = pl.program_id(0)
    lane = lax.broadcasted_iota(jnp.int32, (PEER_SLOTS, tb), 1)
    hi_mask = jnp.uint32(0xFFFF0000)

    def fetch_idx(g, par):
        return pltpu.make_async_copy(idx_hbm.at[step, pl.ds(g * nbuf, nbuf)], idx_sm.at[par], idx_sem.at[par])

    def issue(par, j):
        for k in range(PEER_SLOTS):
            dst = buf.at[pl.ds(j * slot_rows + k * PEER_PITCH, 2 * nsub)]
            pltpu.make_async_copy(tbl_hbm.at[idx_sm[par, j, k]], dst, sem.at[j]).start(priority=k % 2)

    def wait(j):
        rows = pl.ds(0, PEER_SLOTS * 2 * nsub)
        pltpu.make_async_copy(buf.at[rows], buf.at[rows], sem.at[j]).wait()

    def tile(j, tab, s):
        w = buf[pl.ds(j * slot_rows + tab * nsub + s, PEER_SLOTS, stride=PEER_PITCH), :]
        return pltpu.bitcast(w << 16, F32), pltpu.bitcast(w & hi_mask, F32)

    def compute(t, j):
        xrow = h_ref[pl.ds(t, 1), :]
        acc = None
        for s in range(nsub):
            lo, hi = tile(j, 0, s)
            term = lo * xrow[:, s * LANES:(s + 1) * LANES] + hi * xrow[:, half + s * LANES:half + (s + 1) * LANES]
            acc = term if acc is None else acc + term
        hcol = jnp.sum(acc, axis=-1, keepdims=True)
        gwcol = jnp.sum(jnp.where(lane == t, gw_ref[0], 0.0), axis=-1, keepdims=True)
        ccol = gwcol * _gelu_exact(hcol)
        los, his = [], []
        for s in range(nsub):
            lo, hi = tile(j, 1, s)
            los.append(jnp.sum(lo * ccol, axis=0, keepdims=True))
            his.append(jnp.sum(hi * ccol, axis=0, keepdims=True))
        ybuf[pl.ds(t, 1), :] = jnp.concatenate(los + his, axis=-1)

    def run_group(g, par_next, fetch_g):
        fetch_idx(g + 1, par_next).wait()
        if fetch_g is not None:
            fetch_idx(fetch_g, 1 - par_next).start()
        for j in range(nbuf):
            wait(j)
            compute(g * nbuf + j, j)
            issue(par_next, j)

    fetch_idx(0, 0).start()
    fetch_idx(0, 0).wait()
    fetch_idx(1, 1).start()
    for j in range(nbuf):
        issue(0, j)

    def pair(p, carry):
        run_group(2 * p, 1, 2 * p + 2)
        run_group(2 * p + 1, 0, 2 * p + 3)
        return carry

    lax.fori_loop(0, ngroups // 2 - 1, pair, 0)
    run_group(ngroups - 2, 1, None)
    for j in range(nbuf):
        wait(j)
        compute(tb - nbuf + j, j)
    y = ybuf[...]
    yn = y * lax.rsqrt(jnp.mean(y * y, axis=-1, keepdims=True) + NORM_EPS) * nw_ref[...]
    out_ref[...] = xmid_ref[...] + ada_ref[5:6, :] * yn


def _pack_rows_bf16(tbl):
    rows, d = tbl.shape
    half = d // 2
    lo = lax.bitcast_convert_type(tbl[:, :half].astype(BF16), jnp.uint16).astype(jnp.uint32)
    hi = lax.bitcast_convert_type(tbl[:, half:].astype(BF16), jnp.uint16).astype(jnp.uint32)
    return (lo | (hi << 16)).reshape(rows, half // LANES, LANES)


def _pack_expert_tables(down, up):
    return jnp.concatenate([_pack_rows_bf16(down), _pack_rows_bf16(up)], axis=1)


def _peer_gather(idx_t, gw_t, h2, x_mid, ada3, norm_w, tables, seq):
    t, d = h2.shape
    blocks_per_batch = seq // PEER_TB
    nsub = d // 2 // LANES
    assert PEER_TB % (2 * PEER_NBUF) == 0 and 2 * nsub < PEER_PITCH
    return pl.pallas_call(
        _peer_gather_kernel,
        grid=(t // PEER_TB,),
        in_specs=[pl.BlockSpec(memory_space=pl.ANY),
                  pl.BlockSpec((1, PEER_SLOTS, PEER_TB), lambda i: (i, 0, 0)),
                  pl.BlockSpec((PEER_TB, d), lambda i: (i, 0)),
                  pl.BlockSpec((PEER_TB, d), lambda i: (i, 0)),
                  pl.BlockSpec((None, 6, d), lambda i: (i // blocks_per_batch, 0, 0)),
                  pl.BlockSpec((1, d), lambda i: (0, 0)),
                  pl.BlockSpec(memory_space=pl.ANY)],
        out_specs=pl.BlockSpec((PEER_TB, d), lambda i: (i, 0)),
        out_shape=jax.ShapeDtypeStruct((t, d), F32),
        scratch_shapes=[pltpu.VMEM((PEER_NBUF * PEER_SLOTS * PEER_PITCH, LANES), jnp.uint32),
                        pltpu.SemaphoreType.DMA((PEER_NBUF,)),
                        pltpu.SMEM((2, PEER_NBUF, PEER_SLOTS), jnp.int32),
                        pltpu.SemaphoreType.DMA((2,)),
                        pltpu.VMEM((PEER_TB, d), F32)],
        compiler_params=_params("arbitrary"),
        name="peer_gather",
    )(idx_t, gw_t, h2, x_mid, ada3, norm_w, tables)


def _permute_w_in(w_in):
    o_attn, o_dn, o_z = 0, 3 * ATTN_WIDTH, 3 * ATTN_WIDTH + 3 * DN_WIDTH
    o_b = o_z + DN_WIDTH
    o_gates = o_b + 2 * DN_HEADS
    parts = [w_in[:, o_gates:o_gates + 4096], w_in[:, o_dn:o_dn + 3 * DN_WIDTH], w_in[:, o_z:o_z + DN_WIDTH],
             w_in[:, o_attn:o_attn + 3 * ATTN_WIDTH], w_in[:, o_b:o_b + 2 * DN_HEADS]]
    w = jnp.concatenate(parts, axis=1)
    return jnp.pad(w, ((0, 0), (0, PROJ_WIDTH - w.shape[1]))).astype(BF16)


def _layer(x2, c, bsz, seq, w_ada, b_ada, norm_pre_mix, norm_post_mix, norm_pre_ffn, norm_post_ffn, w_in, conv_w,
           a_log, dt_bias, dn_norm_w, w_attn_out, w_delta_out, w_mix_out, peer_w_query, peer_sub_keys, peer_down,
           peer_up, cos_t, sin_t):
    d = x2.shape[1]
    row = lambda v: v.reshape(1, -1)
    ada3 = _ada(c, w_ada, b_ada).reshape(bsz, 6, d)
    proj = _inproj(x2, ada3, row(norm_pre_mix), _permute_w_in(w_in), seq)
    y_attn = _attention(proj, cos_t, sin_t, bsz, seq)
    o_gated = _deltanet(proj, conv_w, a_log, dt_bias, dn_norm_w, bsz, seq)
    x_mid, h2 = _mixout(y_attn, o_gated, proj, x2, ada3, w_attn_out.astype(BF16), w_delta_out.astype(BF16),
                        w_mix_out.astype(BF16), row(norm_post_mix), row(norm_pre_ffn), seq)
    idx_t, gw_t = _peer_route(h2, peer_w_query.astype(BF16), peer_sub_keys)
    return _peer_gather(idx_t, gw_t, h2, x_mid, ada3, row(norm_post_ffn), _pack_expert_tables(peer_down, peer_up),
                        seq)


def kernel(x, c, w_ada, b_ada, norm_pre_mix, norm_post_mix, norm_pre_ffn, norm_post_ffn, w_in, conv_w, a_log, dt_bias, dn_norm_w, w_attn_out, w_delta_out, w_mix_out, peer_w_query, peer_sub_keys, peer_down, peer_up):
    bsz, seq, d = x.shape
    x2 = x.reshape(bsz * seq, d)
    cos_t, sin_t = _rope_tables(seq)
    for layer in range(w_ada.shape[0]):
        x2 = _layer(x2, c, bsz, seq, w_ada[layer], b_ada[layer], norm_pre_mix[layer], norm_post_mix[layer],
                    norm_pre_ffn[layer], norm_post_ffn[layer], w_in[layer], conv_w[layer], a_log[layer],
                    dt_bias[layer], dn_norm_w[layer], w_attn_out[layer], w_delta_out[layer], w_mix_out[layer],
                    peer_w_query[layer], peer_sub_keys[layer], peer_down[layer], peer_up[layer], cos_t, sin_t)
    return x2.reshape(bsz, seq, d)
```

```python
import functools
import math

import jax
import jax.numpy as jnp
from jax import lax
from jax.experimental import pallas as pl
from jax.experimental.pallas import tpu as pltpu

F32 = jnp.float32
BF16 = jnp.bfloat16
HIGHEST = lax.Precision.HIGHEST
LANES = 128
SUBLANES = 8
VMEM_LIMIT = 56 * 1024 * 1024

NORM_EPS = 1e-6
HEAD_DIM = 128
ATTN_GROUPS = ((128, 1), (512, 4), (2048, 16))
ATTN_HPG = 4
ATTN_HEADS = ATTN_HPG * len(ATTN_GROUPS)
ATTN_WIDTH = ATTN_HEADS * HEAD_DIM
ATTN_BLOCK = 128
ATTN_BPI = 4
ROPE_THETA = 500000.0
ROPE_DIM = HEAD_DIM // 4
DN_HEADS = 16
DN_DIM = 128
DN_WIDTH = DN_HEADS * DN_DIM
DN_CONV = 4
DN_CHUNK = 64
DN_HB = 2
DN_CPI = 4
PEER_HEADS = 8
PEER_KEYS = 128
PEER_QDIM = 128
PEER_TOPK = 16
PEER_SLOTS = PEER_HEADS * PEER_TOPK
PEER_HPS = 2
PEER_TB = 128
PEER_NBUF = 8
PEER_PITCH = 17

COL_GA = 0
COL_GD = COL_GA + 2048
COL_DQ = COL_GD + 2048
COL_DK = COL_DQ + DN_WIDTH
COL_DV = COL_DK + DN_WIDTH
COL_DZ = COL_DV + DN_WIDTH
COL_AQ = COL_DZ + DN_WIDTH
COL_AK = COL_AQ + ATTN_WIDTH
COL_AV = COL_AK + ATTN_WIDTH
COL_BA = COL_AV + ATTN_WIDTH
PROJ_WIDTH = 17 * 1024


def _silu(x):
    return x * jax.nn.sigmoid(x)


def _gelu_exact(x):
    return 0.5 * x * (1.0 + lax.erf(x * (2.0 ** -0.5)))


def _dot(a, b, precision=None):
    return jnp.dot(a, b, preferred_element_type=F32, precision=precision)


def _dot_nt(a, b, precision=None):
    return lax.dot_general(a, b, (((1,), (1,)), ((), ())), preferred_element_type=F32, precision=precision)


def _dot_tn(a, b, precision=None):
    return lax.dot_general(a, b, (((0,), (0,)), ((), ())), preferred_element_type=F32, precision=precision)


def _params(*sem):
    return pltpu.CompilerParams(dimension_semantics=sem, vmem_limit_bytes=VMEM_LIMIT)


def _ada_kernel(c_ref, w_ref, b_ref, o_ref):
    o_ref[...] = _dot(_silu(c_ref[...]), w_ref[...], HIGHEST) + b_ref[...]


def _ada(c, w, b):
    bsz, d = c.shape
    n = w.shape[1]
    tn = 1536
    return pl.pallas_call(
        _ada_kernel,
        grid=(n // tn,),
        in_specs=[pl.BlockSpec((bsz, d), lambda j: (0, 0)),
                  pl.BlockSpec((d, tn), lambda j: (0, j)),
                  pl.BlockSpec((1, tn), lambda j: (0, j))],
        out_specs=pl.BlockSpec((bsz, tn), lambda j: (0, j)),
        out_shape=jax.ShapeDtypeStruct((bsz, n), F32),
        compiler_params=_params("parallel"),
        name="ada",
    )(c, w, b.reshape(1, n))


def _rope_kernel(cos_ref, sin_ref):
    rows = cos_ref.shape[0]
    pos = (lax.broadcasted_iota(jnp.int32, (rows, LANES), 0) + pl.program_id(0) * rows).astype(F32)
    lane = lax.broadcasted_iota(jnp.int32, (rows, LANES), 1)
    half = ROPE_DIM // 2
    inv_freq = jnp.exp((lane % half).astype(F32) * (-(2.0 / ROPE_DIM) * math.log(ROPE_THETA)))
    ang = pos * inv_freq
    cos_ref[...] = jnp.where(lane < ROPE_DIM, jnp.cos(ang), 1.0)
    sin = jnp.sin(ang)
    sin_ref[...] = jnp.where(lane < half, -sin, jnp.where(lane < ROPE_DIM, sin, 0.0))


def _rope_tables(seq):
    rows = 256
    return pl.pallas_call(
        _rope_kernel,
        grid=(seq // rows,),
        out_specs=[pl.BlockSpec((rows, LANES), lambda i: (i, 0))] * 2,
        out_shape=[jax.ShapeDtypeStruct((seq, LANES), F32)] * 2,
        compiler_params=_params("parallel"),
        name="rope",
    )()


def _inproj_kernel(x_ref, ada_ref, nw_ref, w_ref, o_ref, h_ref):
    @pl.when(pl.program_id(1) == 0)
    def _():
        x = x_ref[...]
        y = x * lax.rsqrt(jnp.mean(x * x, axis=-1, keepdims=True) + NORM_EPS) * nw_ref[...]
        h_ref[...] = (y * (1.0 + ada_ref[1:2, :]) + ada_ref[0:1, :]).astype(BF16)

    o_ref[...] = _dot(h_ref[...], w_ref[...])


def _inproj(x2, ada3, norm_w, w_bf16, seq):
    t, d = x2.shape
    n = w_bf16.shape[1]
    tm, tn = 1024, 1024
    return pl.pallas_call(
        _inproj_kernel,
        grid=(t // tm, n // tn),
        in_specs=[pl.BlockSpec((tm, d), lambda i, j: (i, 0)),
                  pl.BlockSpec((None, 6, d), lambda i, j: (i * tm // seq, 0, 0)),
                  pl.BlockSpec((1, d), lambda i, j: (0, 0)),
                  pl.BlockSpec((d, tn), lambda i, j: (0, j))],
        out_specs=pl.BlockSpec((tm, tn), lambda i, j: (i, j)),
        out_shape=jax.ShapeDtypeStruct((t, n), F32),
        scratch_shapes=[pltpu.VMEM((tm, d), BF16)],
        compiler_params=_params("parallel", "arbitrary"),
        name="in_proj",
    )(x2, ada3, norm_w, w_bf16)


def _attn_kernel(q1, k1, v1, q2, k2, v2, q3, k3, v3, cos_ref, sin_ref, o_ref, qs, ks, vs, acc_s, m_s, l_s):
    seq = q1.shape[0]
    nblk = seq // ATTN_BLOCK
    half = ROPE_DIM // 2
    pr = lax.broadcasted_iota(jnp.int32, (LANES, LANES), 0)
    pc = lax.broadcasted_iota(jnp.int32, (LANES, LANES), 1)
    perm = (((pr == pc + half) & (pc < half)) | ((pr == pc - half) & (pc >= half) & (pc < ROPE_DIM))).astype(BF16)
    perm2 = jnp.concatenate([perm, perm], axis=0)

    def rotary(t):
        hi, lo = _bf16_parts(t, 2)
        partner = _dot(jnp.concatenate([hi, lo], axis=1).astype(BF16), perm2)
        return t * cos_ref[...] + partner * sin_ref[...]

    qi = lax.broadcasted_iota(jnp.int32, (ATTN_BLOCK, 2 * ATTN_BLOCK), 0)
    kj = lax.broadcasted_iota(jnp.int32, (ATTN_BLOCK, 2 * ATTN_BLOCK), 1)
    dist = qi + ATTN_BLOCK - kj
    scale = HEAD_DIM ** -0.5

    for g, (refs, (window, dil)) in enumerate(zip(((q1, k1, v1), (q2, k2, v2), (q3, k3, v3)), ATTN_GROUPS)):
        q_ref, k_ref, v_ref = refs
        n_back = window // dil
        pad = ATTN_BLOCK * dil
        in_window = (dist >= 0) & (dist <= n_back)
        qs[...] = rotary(q_ref[...])
        ks[0:pad, :] = jnp.zeros((pad, LANES), F32)
        vs[0:pad, :] = jnp.zeros((pad, LANES), F32)
        ks[pad:pad + seq, :] = rotary(k_ref[...])
        vs[pad:pad + seq, :] = v_ref[...]

        def rows_of(idx, dil=dil, pad=pad):
            res = lax.rem(idx, dil)
            nb = idx // dil
            start = res + nb * pad
            if dil == 1:
                return nb, pl.ds(start, ATTN_BLOCK), pl.ds(start, 2 * ATTN_BLOCK)
            return nb, pl.ds(start, ATTN_BLOCK, stride=dil), pl.ds(start, 2 * ATTN_BLOCK, stride=dil)

        def blocks(i, carry, g=g, rows_of=rows_of, in_window=in_window):
            rows = [rows_of(i * ATTN_BPI + j) for j in range(ATTN_BPI)]
            ss = [_dot_nt(qs[rq, :], ks[rk, :]) * scale for _, rq, rk in rows]
            ps, ms = [], []
            for (nb, _, _), s in zip(rows, ss):
                s = jnp.where(in_window & ((nb > 0) | (kj >= ATTN_BLOCK)), s, -jnp.inf)
                m = jnp.max(s, axis=-1, keepdims=True)
                ms.append(m)
                ps.append(jnp.exp(s - m))
            accs = [_dot(p, vs[rk, :]) for (_, _, rk), p in zip(rows, ps)]
            for (_, rq, _), acc, m, p in zip(rows, accs, ms, ps):
                acc_s[g, rq, :] = acc
                m_s[g, rq, :] = jnp.broadcast_to(m, (ATTN_BLOCK, LANES))
                l_s[g, rq, :] = jnp.broadcast_to(jnp.sum(p, axis=-1, keepdims=True), (ATTN_BLOCK, LANES))
            return carry

        lax.fori_loop(0, nblk // ATTN_BPI, blocks, 0)

    def merge(i, carry):
        rows = pl.ds(pl.multiple_of(i * ATTN_BLOCK, ATTN_BLOCK), ATTN_BLOCK)
        ms = [m_s[g, rows, :] for g in range(3)]
        mx = jnp.maximum(jnp.maximum(ms[0], ms[1]), ms[2])
        ws = [jnp.exp(m - mx) for m in ms]
        num = ws[0] * acc_s[0, rows, :] + ws[1] * acc_s[1, rows, :] + ws[2] * acc_s[2, rows, :]
        den = ws[0] * l_s[0, rows, :] + ws[1] * l_s[1, rows, :] + ws[2] * l_s[2, rows, :]
        o_ref[rows, :] = (num / den).astype(o_ref.dtype)
        return carry

    lax.fori_loop(0, nblk, merge, 0)


def _attention(proj, cos_t, sin_t, bsz, seq):
    for window, dil in ATTN_GROUPS:
        assert window // dil == ATTN_BLOCK and seq % (ATTN_BLOCK * dil) == 0
    pad_max = ATTN_BLOCK * max(d for _, d in ATTN_GROUPS)

    def head_spec(col0, g):
        blk0 = col0 // HEAD_DIM + g * ATTN_HPG
        return pl.BlockSpec((seq, HEAD_DIM), lambda b, h: (b, blk0 + h))

    in_specs = []
    for g in range(len(ATTN_GROUPS)):
        in_specs += [head_spec(COL_AQ, g), head_spec(COL_AK, g), head_spec(COL_AV, g)]
    in_specs += [pl.BlockSpec((seq, LANES), lambda b, h: (0, 0))] * 2
    return pl.pallas_call(
        _attn_kernel,
        grid=(bsz, ATTN_HPG),
        in_specs=in_specs,
        out_specs=pl.BlockSpec((seq, HEAD_DIM), lambda b, h: (b, h)),
        out_shape=jax.ShapeDtypeStruct((bsz * seq, ATTN_HPG * HEAD_DIM), BF16),
        scratch_shapes=[pltpu.VMEM((seq, LANES), F32),
                        pltpu.VMEM((seq + pad_max, LANES), F32),
                        pltpu.VMEM((seq + pad_max, LANES), F32),
                        pltpu.VMEM((3, seq, LANES), F32),
                        pltpu.VMEM((3, seq, LANES), F32),
                        pltpu.VMEM((3, seq, LANES), F32)],
        compiler_params=_params("parallel", "parallel"),
        name="attention",
    )(*([proj] * 9), cos_t, sin_t)


def _bf16_parts(a, n):
    parts, rest = [], a
    for _ in range(n):
        piece = rest.astype(BF16).astype(F32)
        parts.append(piece)
        rest = rest - piece
    return parts


def _lhs3(a):
    hi, lo = _bf16_parts(a, 2)
    return jnp.concatenate([hi, hi, lo], axis=1).astype(BF16)


def _rhs3(b):
    hi, lo = _bf16_parts(b, 2)
    return jnp.concatenate([hi, lo, hi], axis=0).astype(BF16)


def _split3_rows(b):
    return jnp.concatenate(_bf16_parts(b, 3), axis=0).astype(BF16)


def _deltanet_kernel(q_ref, k_ref, v_ref, z_ref, ba_ref, cwq_ref, cwk_ref, cwv_ref, alog_ref, dtb_ref, nw_ref,
                     o_ref, kq_s, b_s, o_s, cd_s):
    seq = q_ref.shape[0]
    nchunk = seq // DN_CHUNK
    c64 = DN_CHUNK
    ri = lax.broadcasted_iota(jnp.int32, (c64, c64), 0)
    ci = lax.broadcasted_iota(jnp.int32, (c64, c64), 1)
    lower_incl = ri >= ci
    lower_strict = ri > ci
    ones_lower = lower_incl.astype(F32)
    eye = (ri == ci).astype(F32)
    lane = lax.broadcasted_iota(jnp.int32, (c64, LANES), 1)
    lane16 = lax.broadcasted_iota(jnp.int32, (1, DN_HEADS), 1)

    eye_bf16 = eye.astype(BF16)
    ones3 = jnp.concatenate([ones_lower] * 3, axis=1).astype(BF16)

    def conv_silu(x_ref, cw_ref, hh, r0, c):
        cols = slice(hh * DN_DIM, (hh + 1) * DN_DIM)
        prev = x_ref[pl.ds(pl.multiple_of(jnp.maximum(r0 - SUBLANES, 0), SUBLANES), SUBLANES), cols]
        prev = jnp.where(c > 0, prev, 0.0)
        win = jnp.concatenate([prev, x_ref[pl.ds(r0, c64), cols]], axis=0)
        y = win[SUBLANES:, :] * cw_ref[DN_CONV - 1:DN_CONV, cols]
        for j in range(1, DN_CONV):
            y = y + pltpu.roll(win, j, 0)[SUBLANES:, :] * cw_ref[DN_CONV - 1 - j:DN_CONV - j, cols]
        return _silu(y)

    def prepare(c, hh):
        r0 = pl.multiple_of(c * c64, c64)
        head = pl.program_id(1) * DN_HB + hh
        qx = conv_silu(q_ref, cwq_ref, hh, r0, c)
        kx = conv_silu(k_ref, cwk_ref, hh, r0, c)
        vx = conv_silu(v_ref, cwv_ref, hh, r0, c)
        qn = qx * lax.rsqrt(jnp.sum(qx * qx, axis=-1, keepdims=True) + 1e-6) * (DN_DIM ** -0.5)
        kn = kx * lax.rsqrt(jnp.sum(kx * kx, axis=-1, keepdims=True) + 1e-6)
        ba = ba_ref[pl.ds(r0, c64), :]
        b_col = jnp.sum(jnp.where(lane == head, ba, 0.0), axis=-1, keepdims=True)
        a_col = jnp.sum(jnp.where(lane == head + DN_HEADS, ba, 0.0), axis=-1, keepdims=True)
        a_log = jnp.sum(jnp.where(lane16 == head, alog_ref[...], 0.0), axis=-1, keepdims=True)
        dt_b = jnp.sum(jnp.where(lane16 == head, dtb_ref[...], 0.0), axis=-1, keepdims=True)
        beta = jax.nn.sigmoid(b_col)
        ax = a_col + dt_b
        softplus = jnp.maximum(ax, 0.0) + jnp.log1p(jnp.exp(-jnp.abs(ax)))
        g = -jnp.exp(a_log) * softplus
        gm = jnp.where(lower_strict, jnp.broadcast_to(g, (c64, c64)), 0.0)
        kb = kn * beta
        return dict(qn=qn, kn=kn, vb=vx * beta, kb=kb, g=g, gm3=_split3_rows(gm),
                    kbqn=jnp.concatenate([kb, qn], axis=0).astype(BF16), kn16=kn.astype(BF16))

    def local(i, carry):
        todo = [(i * DN_CPI + j, hh) for j in range(DN_CPI) for hh in range(DN_HB)]
        ps = [prepare(c, hh) for c, hh in todo]
        diffs = [_dot(ones3, p["gm3"]) for p in ps]
        scores = [_dot_nt(p["kbqn"], p["kn16"]) for p in ps]
        mids = []
        for p, diff, sc in zip(ps, diffs, scores):
            gam = diff[:, 0:1] + p["g"][0:1, :]
            gam_last = gam[c64 - 1:c64, :]
            decay = jnp.where(lower_incl, jnp.exp(jnp.where(lower_incl, diff, 0.0)), 0.0)
            eg = jnp.exp(gam)
            mids.append(dict(nmat=-jnp.where(lower_strict, sc[:c64] * decay, 0.0), qk=sc[c64:] * decay,
                             rhs=jnp.concatenate([p["vb"], p["kb"] * eg], axis=-1), qd=p["qn"] * eg,
                             kd=(p["kn"] * jnp.exp(gam_last - gam)).astype(BF16),
                             cd=jnp.broadcast_to(jnp.exp(gam_last), (SUBLANES, LANES))))
        invs = [eye + m["nmat"] for m in mids]
        pws = [_dot(_lhs3(m["nmat"]), _rhs3(m["nmat"])) for m in mids]
        for _ in range(4):
            boths = [_dot(_lhs3(pw), _rhs3(jnp.concatenate([inv, pw], axis=1))) for inv, pw in zip(invs, pws)]
            invs = [inv + b[:, :c64] for inv, b in zip(invs, boths)]
            pws = [b[:, c64:] for b in boths]
        ys = [_dot(_lhs3(inv), _rhs3(m["rhs"])) for inv, m in zip(invs, mids)]
        sols = [y + _dot(_lhs3(pw), _rhs3(y)) for y, pw in zip(ys, pws)]
        kdts = [_dot_tn(m["kd"], eye_bf16) for m in mids]
        prods = [_dot(jnp.concatenate([kd_t, m["qk"]], axis=0).astype(BF16),
                      jnp.concatenate([sol[:, DN_DIM:], sol[:, :DN_DIM]], axis=1).astype(BF16))
                 for m, sol, kd_t in zip(mids, sols, kdts)]
        for (c, hh), m, pr in zip(todo, mids, prods):
            kq_s[hh, pl.ds(pl.multiple_of(c * (DN_DIM + c64), c64), DN_DIM + c64), :] = jnp.concatenate(
                [pr[:DN_DIM, :DN_DIM], m["qd"] - pr[DN_DIM:, :DN_DIM]], axis=0).astype(BF16)
            b_s[hh, pl.ds(pl.multiple_of(c * DN_DIM, DN_DIM), DN_DIM), :] = pr[:DN_DIM, DN_DIM:]
            o_s[hh, pl.ds(pl.multiple_of(c * c64, c64), c64), :] = pr[DN_DIM:, DN_DIM:]
            cd_s[hh, pl.ds(pl.multiple_of(c * SUBLANES, SUBLANES), SUBLANES), :] = m["cd"]
        return carry

    lax.fori_loop(0, nchunk // DN_CPI, local, 0)

    def scan(c, states):
        r0 = pl.multiple_of(c * c64, c64)
        rows = pl.ds(r0, c64)
        heads = range(DN_HB)
        rkq = pl.ds(pl.multiple_of(c * (DN_DIM + c64), c64), DN_DIM + c64)
        rb = pl.ds(pl.multiple_of(c * DN_DIM, DN_DIM), DN_DIM)
        cds = [cd_s[hh, pl.ds(pl.multiple_of(c * SUBLANES, SUBLANES), SUBLANES), :][0:1, :] for hh in heads]
        rs = [_dot(kq_s[hh, rkq, :], states[hh].astype(BF16)) for hh in heads]
        new_states = [states[hh] * cds[hh] - rs[hh][:DN_DIM] + b_s[hh, rb, :] for hh in heads]
        outs = []
        for hh in heads:
            o = rs[hh][DN_DIM:] + o_s[hh, rows, :]
            on = o * lax.rsqrt(jnp.mean(o * o, axis=-1, keepdims=True) + NORM_EPS) * nw_ref[...]
            outs.append((on * _silu(z_ref[rows, hh * DN_DIM:(hh + 1) * DN_DIM])).astype(o_ref.dtype))
        for hh in heads:
            o_ref[rows, hh * DN_DIM:(hh + 1) * DN_DIM] = outs[hh]
        return tuple(new_states)

    lax.fori_loop(0, nchunk, scan, tuple(jnp.zeros((DN_DIM, DN_DIM), F32) for _ in range(DN_HB)))


def _deltanet(proj, conv_w, a_log, dt_bias, norm_w, bsz, seq):
    wb = DN_DIM * DN_HB
    hblocks = DN_HEADS // DN_HB
    nchunk = seq // DN_CHUNK
    assert nchunk % DN_CPI == 0

    def act_spec(col0):
        blk0 = col0 // wb
        return pl.BlockSpec((seq, wb), lambda b, h: (b, blk0 + h))

    def cw_spec(part):
        blk0 = part * DN_WIDTH // wb
        return pl.BlockSpec((DN_CONV, wb), lambda b, h: (0, blk0 + h))

    return pl.pallas_call(
        _deltanet_kernel,
        grid=(bsz, hblocks),
        in_specs=[act_spec(COL_DQ), act_spec(COL_DK), act_spec(COL_DV), act_spec(COL_DZ),
                  pl.BlockSpec((seq, LANES), lambda b, h: (b, COL_BA // LANES)),
                  cw_spec(0), cw_spec(1), cw_spec(2),
                  pl.BlockSpec((1, DN_HEADS), lambda b, h: (0, 0)),
                  pl.BlockSpec((1, DN_HEADS), lambda b, h: (0, 0)),
                  pl.BlockSpec((1, DN_DIM), lambda b, h: (0, 0))],
        out_specs=pl.BlockSpec((seq, wb), lambda b, h: (b, h)),
        out_shape=jax.ShapeDtypeStruct((bsz * seq, DN_WIDTH), BF16),
        scratch_shapes=[pltpu.VMEM((DN_HB, nchunk * (DN_DIM + DN_CHUNK), DN_DIM), BF16),
                        pltpu.VMEM((DN_HB, nchunk * DN_DIM, DN_DIM), F32),
                        pltpu.VMEM((DN_HB, seq, DN_DIM), F32),
                        pltpu.VMEM((DN_HB, nchunk * SUBLANES, LANES), F32)],
        compiler_params=_params("parallel", "parallel"),
        name="deltanet",
    )(proj, proj, proj, proj, proj, conv_w, conv_w, conv_w,
      a_log.reshape(1, DN_HEADS), dt_bias.reshape(1, DN_HEADS), norm_w.reshape(1, DN_DIM))


def _mixout_kernel(ya_ref, og_ref, ga_ref, gd_ref, x_ref, ada_ref, wa_ref, wd_ref, wm_ref, npost_ref, npre_ref,
                   xmid_ref, h2_ref):
    y_attn = _dot(ya_ref[...], wa_ref[...])
    y_delta = _dot(og_ref[...], wd_ref[...])
    merged = jax.nn.sigmoid(ga_ref[...]) * y_attn + jax.nn.sigmoid(gd_ref[...]) * y_delta
    y = _dot(merged.astype(BF16), wm_ref[...])
    yn = y * lax.rsqrt(jnp.mean(y * y, axis=-1, keepdims=True) + NORM_EPS) * npost_ref[...]
    x_mid = x_ref[...] + ada_ref[2:3, :] * yn
    xmid_ref[...] = x_mid
    hn = x_mid * lax.rsqrt(jnp.mean(x_mid * x_mid, axis=-1, keepdims=True) + NORM_EPS) * npre_ref[...]
    h2_ref[...] = hn * (1.0 + ada_ref[4:5, :]) + ada_ref[3:4, :]


def _mixout(y_attn, o_gated, proj, x2, ada3, w_attn, w_delta, w_mix, norm_post, norm_pre_ffn, seq):
    t, d = x2.shape
    tm = 256
    const = dict(pipeline_mode=pl.Buffered(1))
    return pl.pallas_call(
        _mixout_kernel,
        grid=(t // tm,),
        in_specs=[pl.BlockSpec((tm, y_attn.shape[1]), lambda i: (i, 0)),
                  pl.BlockSpec((tm, d), lambda i: (i, 0)),
                  pl.BlockSpec((tm, d), lambda i: (i, COL_GA // d)),
                  pl.BlockSpec((tm, d), lambda i: (i, COL_GD // d)),
                  pl.BlockSpec((tm, d), lambda i: (i, 0)),
                  pl.BlockSpec((None, 6, d), lambda i: (i * tm // seq, 0, 0)),
                  pl.BlockSpec(w_attn.shape, lambda i: (0, 0), **const),
                  pl.BlockSpec(w_delta.shape, lambda i: (0, 0), **const),
                  pl.BlockSpec(w_mix.shape, lambda i: (0, 0), **const),
                  pl.BlockSpec((1, d), lambda i: (0, 0)),
                  pl.BlockSpec((1, d), lambda i: (0, 0))],
        out_specs=[pl.BlockSpec((tm, d), lambda i: (i, 0))] * 2,
        out_shape=[jax.ShapeDtypeStruct((t, d), F32)] * 2,
        compiler_params=_params("parallel"),
        name="mix_out",
    )(y_attn, o_gated, proj, proj, x2, ada3, w_attn, w_delta, w_mix, norm_post, norm_pre_ffn)


def _topk_rows(s, k):
    n = s.shape[0]
    rid = lax.broadcasted_iota(jnp.int32, s.shape, 0)
    vals, ids = [], []
    for _ in range(k):
        m = jnp.max(s, axis=0, keepdims=True)
        sel = jnp.min(jnp.where(s == m, rid, n), axis=0, keepdims=True)
        vals.append(m)
        ids.append(sel)
        s = jnp.where(rid == sel, -jnp.inf, s)
    return jnp.concatenate(vals, axis=0), jnp.concatenate(ids, axis=0)


def _peer_route_kernel(h_ref, wq_ref, keys_ref, idx_ref, gw_ref, q_s, idx_s, gw_s):
    tm = h_ref.shape[0]
    hblock = pl.program_id(1)

    @pl.when(hblock == 0)
    def _():
        q = _dot(h_ref[...].astype(BF16), wq_ref[...])
        for hh in range(PEER_HEADS):
            q_s[hh] = q[:, hh * PEER_QDIM:(hh + 1) * PEER_QDIM]

    for hh in range(PEER_HPS):
        _route_head(hblock * PEER_HPS + hh, q_s, keys_ref.at[hh], idx_s, gw_s, tm)

    @pl.when(hblock == PEER_HEADS // PEER_HPS - 1)
    def _():
        for bb in range(tm // PEER_TB):
            tok = slice(bb * PEER_TB, (bb + 1) * PEER_TB)
            gw_ref[bb] = gw_s[:, tok]
            idx_ref[bb] = idx_s[:, tok].T


def _route_head(head, q_s, keys_ref, idx_s, gw_s, tm):
    half = PEER_QDIM // 2
    qh = q_s[head]
    tops = []
    for p in range(2):
        keys_hi, keys_lo = _bf16_parts(keys_ref[p], 2)
        q_hi, q_lo = _bf16_parts(qh[:, p * half:(p + 1) * half], 2)
        s = _dot_nt(jnp.concatenate([keys_hi, keys_hi, keys_lo], axis=1).astype(BF16),
                    jnp.concatenate([q_hi, q_lo, q_hi], axis=1).astype(BF16))
        tops.append(_topk_rows(s, PEER_TOPK))
    (s0, i0), (s1, i1) = tops
    counts = [PEER_TOPK // (a + 1) for a in range(PEER_TOPK)]
    n_cand = sum(counts)
    n_pad = -n_cand % SUBLANES
    cand = jnp.concatenate([s0[a:a + 1, :] + s1[0:counts[a], :] for a in range(PEER_TOPK)]
                           + [jnp.full((n_pad, tm), -jnp.inf, F32)], axis=0)
    best_s, best_row = _topk_rows(cand, PEER_TOPK)
    a_sel = jnp.zeros_like(best_row)
    b_sel = best_row
    start = 0
    for a in range(PEER_TOPK - 1):
        start += counts[a]
        past = (best_row >= start).astype(jnp.int32)
        a_sel = a_sel + past
        b_sel = b_sel - past * counts[a]
    e0 = jnp.zeros_like(best_row)
    e1 = jnp.zeros_like(best_row)
    for a in range(PEER_TOPK):
        e0 = e0 + jnp.where(a_sel == a, i0[a:a + 1, :], 0)
        e1 = e1 + jnp.where(b_sel == a, i1[a:a + 1, :], 0)
    ex = jnp.exp(best_s - jnp.max(best_s, axis=0, keepdims=True))
    rows = pl.ds(pl.multiple_of(head * PEER_TOPK, PEER_TOPK), PEER_TOPK)
    idx_s[rows, :] = e0 * PEER_KEYS + e1
    gw_s[rows, :] = ex / jnp.sum(ex, axis=0, keepdims=True)


def _peer_route(h2, w_query_bf16, keys):
    t, d = h2.shape
    tm = 256
    nb = tm // PEER_TB
    return pl.pallas_call(
        _peer_route_kernel,
        grid=(t // tm, PEER_HEADS // PEER_HPS),
        in_specs=[pl.BlockSpec((tm, d), lambda i, h: (i, 0)),
                  pl.BlockSpec(w_query_bf16.shape, lambda i, h: (0, 0)),
                  pl.BlockSpec((PEER_HPS, 2, PEER_KEYS, PEER_QDIM // 2), lambda i, h: (h, 0, 0, 0))],
        out_specs=[pl.BlockSpec((nb, PEER_TB, PEER_SLOTS), lambda i, h: (i, 0, 0)),
                   pl.BlockSpec((nb, PEER_SLOTS, PEER_TB), lambda i, h: (i, 0, 0))],
        out_shape=[jax.ShapeDtypeStruct((t // PEER_TB, PEER_TB, PEER_SLOTS), jnp.int32),
                   jax.ShapeDtypeStruct((t // PEER_TB, PEER_SLOTS, PEER_TB), F32)],
        scratch_shapes=[pltpu.VMEM((PEER_HEADS, tm, PEER_QDIM), F32),
                        pltpu.VMEM((PEER_SLOTS, tm), jnp.int32),
                        pltpu.VMEM((PEER_SLOTS, tm), F32)],
        compiler_params=_params("parallel", "arbitrary"),
        name="peer_route",
    )(h2, w_query_bf16, keys)


def _peer_gather_kernel(idx_hbm, gw_ref, h_ref, xmid_ref, ada_ref, nw_ref, tbl_hbm,
                        out_ref, buf, sem, idx_sm, idx_sem, ybuf):
    tb, d = h_ref.shape
    half = d // 2
    nsub = half // LANES
    slot_rows = PEER_SLOTS * PEER_PITCH
    nbuf = PEER_NBUF
    ngroups = tb // nbuf
    step = pl.program_id(0)
    lane = lax.broadcasted_iota(jnp.int32, (PEER_SLOTS, tb), 1)
    hi_mask = jnp.uint32(0xFFFF0000)

    def fetch_idx(g, par):
        return pltpu.make_async_copy(idx_hbm.at[step, pl.ds(g * nbuf, nbuf)], idx_sm.at[par], idx_sem.at[par])

    def issue(par, j):
        for k in range(PEER_SLOTS):
            dst = buf.at[pl.ds(j * slot_rows + k * PEER_PITCH, 2 * nsub)]
            pltpu.make_async_copy(tbl_hbm.at[idx_sm[par, j, k]], dst, sem.at[j]).start(priority=k % 2)

    def wait(j):
        rows = pl.ds(0, PEER_SLOTS * 2 * nsub)
        pltpu.make_async_copy(buf.at[rows], buf.at[rows], sem.at[j]).wait()

    def tile(j, tab, s):
        w = buf[pl.ds(j * slot_rows + tab * nsub + s, PEER_SLOTS, stride=PEER_PITCH), :]
        return pltpu.bitcast(w << 16, F32), pltpu.bitcast(w & hi_mask, F32)

    def compute(t, j):
        xrow = h_ref[pl.ds(t, 1), :]
        acc = None
        for s in range(nsub):
            lo, hi = tile(j, 0, s)
            term = lo * xrow[:, s * LANES:(s + 1) * LANES] + hi * xrow[:, half + s * LANES:half + (s + 1) * LANES]
            acc = term if acc is None else acc + term
        hcol = jnp.sum(acc, axis=-1, keepdims=True)
        gwcol = jnp.sum(jnp.where(lane == t, gw_ref[0], 0.0), axis=-1, keepdims=True)
        ccol = gwcol * _gelu_exact(hcol)
        los, his = [], []
        for s in range(nsub):
            lo, hi = tile(j, 1, s)
            los.append(jnp.sum(lo * ccol, axis=0, keepdims=True))
            his.append(jnp.sum(hi * ccol, axis=0, keepdims=True))
        ybuf[pl.ds(t, 1), :] = jnp.concatenate(los + his, axis=-1)

    def run_group(g, par_next, fetch_g):
        fetch_idx(g + 1, par_next).wait()
        if fetch_g is not None:
            fetch_idx(fetch_g, 1 - par_next).start()
        for j in range(nbuf):
            wait(j)
            compute(g * nbuf + j, j)
            issue(par_next, j)

    is_first = step == 0
    has_next = step + 1 < pl.num_programs(0)

    def fetch_next_step(g, par):
        return pltpu.make_async_copy(idx_hbm.at[step + 1, pl.ds(g * nbuf, nbuf)], idx_sm.at[par], idx_sem.at[par])

    @pl.when(is_first)
    def _():
        fetch_idx(0, 0).start()
        fetch_idx(0, 0).wait()
        fetch_idx(1, 1).start()
        for j in range(nbuf):
            issue(0, j)

    def pair(p, carry):
        run_group(2 * p, 1, 2 * p + 2)
        run_group(2 * p + 1, 0, 2 * p + 3)
        return carry

    lax.fori_loop(0, ngroups // 2 - 1, pair, 0)

    @pl.when(has_next)
    def _():
        fetch_next_step(0, 0).start()

    run_group(ngroups - 2, 1, None)

    @pl.when(has_next)
    def _():
        fetch_next_step(0, 0).wait()
        for j in range(nbuf):
            wait(j)
            compute(tb - nbuf + j, j)
            issue(0, j)
        fetch_next_step(1, 1).start()

    @pl.when(jnp.logical_not(has_next))
    def _():
        for j in range(nbuf):
            wait(j)
            compute(tb - nbuf + j, j)

    y = ybuf[...]
    yn = y * lax.rsqrt(jnp.mean(y * y, axis=-1, keepdims=True) + NORM_EPS) * nw_ref[...]
    out_ref[...] = xmid_ref[...] + ada_ref[5:6, :] * yn


def _pack_tables_kernel(down_ref, up_ref, o_ref):
    half = down_ref.shape[1] // 2
    hi_mask = jnp.uint32(0xFFFF0000)

    def pack(x):
        lo = pltpu.bitcast(x[:, :half].astype(BF16).astype(F32), jnp.uint32)
        hi = pltpu.bitcast(x[:, half:].astype(BF16).astype(F32), jnp.uint32)
        return (lo >> 16) | (hi & hi_mask)

    o_ref[:, :half] = pack(down_ref[...])
    o_ref[:, half:] = pack(up_ref[...])


def _pack_expert_tables(down, up):
    rows, d = down.shape
    tr = 256
    packed = pl.pallas_call(
        _pack_tables_kernel,
        grid=(rows // tr,),
        in_specs=[pl.BlockSpec((tr, d), lambda i: (i, 0))] * 2,
        out_specs=pl.BlockSpec((tr, d), lambda i: (i, 0)),
        out_shape=jax.ShapeDtypeStruct((rows, d), jnp.uint32),
        compiler_params=_params("parallel"),
        name="pack_tables",
    )(down, up)
    return packed.reshape(rows, d // LANES, LANES)


def _peer_gather(idx_t, gw_t, h2, x_mid, ada3, norm_w, tables, seq):
    t, d = h2.shape
    blocks_per_batch = seq // PEER_TB
    nsub = d // 2 // LANES
    assert PEER_TB % (2 * PEER_NBUF) == 0 and 2 * nsub < PEER_PITCH
    return pl.pallas_call(
        _peer_gather_kernel,
        grid=(t // PEER_TB,),
        in_specs=[pl.BlockSpec(memory_space=pl.ANY),
                  pl.BlockSpec((1, PEER_SLOTS, PEER_TB), lambda i: (i, 0, 0)),
                  pl.BlockSpec((PEER_TB, d), lambda i: (i, 0)),
                  pl.BlockSpec((PEER_TB, d), lambda i: (i, 0)),
                  pl.BlockSpec((None, 6, d), lambda i: (i // blocks_per_batch, 0, 0)),
                  pl.BlockSpec((1, d), lambda i: (0, 0)),
                  pl.BlockSpec(memory_space=pl.ANY)],
        out_specs=pl.BlockSpec((PEER_TB, d), lambda i: (i, 0)),
        out_shape=jax.ShapeDtypeStruct((t, d), F32),
        scratch_shapes=[pltpu.VMEM((PEER_NBUF * PEER_SLOTS * PEER_PITCH, LANES), jnp.uint32),
                        pltpu.SemaphoreType.DMA((PEER_NBUF,)),
                        pltpu.SMEM((2, PEER_NBUF, PEER_SLOTS), jnp.int32),
                        pltpu.SemaphoreType.DMA((2,)),
                        pltpu.VMEM((PEER_TB, d), F32)],
        compiler_params=_params("arbitrary"),
        name="peer_gather",
    )(idx_t, gw_t, h2, x_mid, ada3, norm_w, tables)


def _permute_w_in(w_in):
    o_attn, o_dn, o_z = 0, 3 * ATTN_WIDTH, 3 * ATTN_WIDTH + 3 * DN_WIDTH
    o_b = o_z + DN_WIDTH
    o_gates = o_b + 2 * DN_HEADS
    parts = [w_in[:, o_gates:o_gates + 4096], w_in[:, o_dn:o_dn + 3 * DN_WIDTH], w_in[:, o_z:o_z + DN_WIDTH],
             w_in[:, o_attn:o_attn + 3 * ATTN_WIDTH], w_in[:, o_b:o_b + 2 * DN_HEADS]]
    w = jnp.concatenate(parts, axis=1)
    return jnp.pad(w, ((0, 0), (0, PROJ_WIDTH - w.shape[1]))).astype(BF16)


def _layer(x2, c, bsz, seq, w_ada, b_ada, norm_pre_mix, norm_post_mix, norm_pre_ffn, norm_post_ffn, w_in, conv_w,
           a_log, dt_bias, dn_norm_w, w_attn_out, w_delta_out, w_mix_out, peer_w_query, peer_sub_keys, peer_down,
           peer_up, cos_t, sin_t):
    d = x2.shape[1]
    row = lambda v: v.reshape(1, -1)
    ada3 = _ada(c, w_ada, b_ada).reshape(bsz, 6, d)
    proj = _inproj(x2, ada3, row(norm_pre_mix), _permute_w_in(w_in), seq)
    y_attn = _attention(proj, cos_t, sin_t, bsz, seq)
    o_gated = _deltanet(proj, conv_w, a_log, dt_bias, dn_norm_w, bsz, seq)
    x_mid, h2 = _mixout(y_attn, o_gated, proj, x2, ada3, w_attn_out.astype(BF16), w_delta_out.astype(BF16),
                        w_mix_out.astype(BF16), row(norm_post_mix), row(norm_pre_ffn), seq)
    idx_t, gw_t = _peer_route(h2, peer_w_query.astype(BF16), peer_sub_keys)
    return _peer_gather(idx_t, gw_t, h2, x_mid, ada3, row(norm_post_ffn), _pack_expert_tables(peer_down, peer_up),
                        seq)


def kernel(x, c, w_ada, b_ada, norm_pre_mix, norm_post_mix, norm_pre_ffn, norm_post_ffn, w_in, conv_w, a_log, dt_bias, dn_norm_w, w_attn_out, w_delta_out, w_mix_out, peer_w_query, peer_sub_keys, peer_down, peer_up):
    bsz, seq, d = x.shape
    x2 = x.reshape(bsz * seq, d)
    cos_t, sin_t = _rope_tables(seq)
    for layer in range(w_ada.shape[0]):
        x2 = _layer(x2, c, bsz, seq, w_ada[layer], b_ada[layer], norm_pre_mix[layer], norm_post_mix[layer],
                    norm_pre_ffn[layer], norm_post_ffn[layer], w_in[layer], conv_w[layer], a_log[layer],
                    dt_bias[layer], dn_norm_w[layer], w_attn_out[layer], w_delta_out[layer], w_mix_out[layer],
                    peer_w_query[layer], peer_sub_keys[layer], peer_down[layer], peer_up[layer], cos_t, sin_t)
    return x2.reshape(bsz, seq, d)
```

```python
import functools
import math

import jax
import jax.numpy as jnp
from jax import lax
from jax.experimental import pallas as pl
from jax.experimental.pallas import tpu as pltpu

F32 = jnp.float32
BF16 = jnp.bfloat16
HIGHEST = lax.Precision.HIGHEST
LANES = 128
SUBLANES = 8
VMEM_LIMIT = 56 * 1024 * 1024

NORM_EPS = 1e-6
HEAD_DIM = 128
ATTN_GROUPS = ((128, 1), (512, 4), (2048, 16))
ATTN_HPG = 4
ATTN_HEADS = ATTN_HPG * len(ATTN_GROUPS)
ATTN_WIDTH = ATTN_HEADS * HEAD_DIM
ATTN_BLOCK = 128
ATTN_BPI = 4
ROPE_THETA = 500000.0
ROPE_DIM = HEAD_DIM // 4
DN_HEADS = 16
DN_DIM = 128
DN_WIDTH = DN_HEADS * DN_DIM
DN_CONV = 4
DN_CHUNK = 64
DN_HB = 2
DN_CPI = 4
PEER_HEADS = 8
PEER_KEYS = 128
PEER_QDIM = 128
PEER_TOPK = 16
PEER_SLOTS = PEER_HEADS * PEER_TOPK
PEER_HPS = 2
PEER_TB = 128
PEER_NBUF = 8
PEER_PITCH = 17

COL_GA = 0
COL_GD = COL_GA + 2048
COL_DQ = COL_GD + 2048
COL_DK = COL_DQ + DN_WIDTH
COL_DV = COL_DK + DN_WIDTH
COL_DZ = COL_DV + DN_WIDTH
COL_AQ = COL_DZ + DN_WIDTH
COL_AK = COL_AQ + ATTN_WIDTH
COL_AV = COL_AK + ATTN_WIDTH
COL_BA = COL_AV + ATTN_WIDTH
PROJ_WIDTH = 17 * 1024


def _silu(x):
    return x * jax.nn.sigmoid(x)


def _gelu_exact(x):
    return 0.5 * x * (1.0 + lax.erf(x * (2.0 ** -0.5)))


def _dot(a, b, precision=None):
    return jnp.dot(a, b, preferred_element_type=F32, precision=precision)


def _dot_nt(a, b, precision=None):
    return lax.dot_general(a, b, (((1,), (1,)), ((), ())), preferred_element_type=F32, precision=precision)


def _dot_tn(a, b, precision=None):
    return lax.dot_general(a, b, (((0,), (0,)), ((), ())), preferred_element_type=F32, precision=precision)


def _params(*sem):
    return pltpu.CompilerParams(dimension_semantics=sem, vmem_limit_bytes=VMEM_LIMIT)


def _ada_kernel(c_ref, w_ref, b_ref, o_ref):
    o_ref[...] = _dot(_silu(c_ref[...]), w_ref[...], HIGHEST) + b_ref[...]


def _ada(c, w, b):
    bsz, d = c.shape
    n = w.shape[1]
    tn = 1536
    return pl.pallas_call(
        _ada_kernel,
        grid=(n // tn,),
        in_specs=[pl.BlockSpec((bsz, d), lambda j: (0, 0)),
                  pl.BlockSpec((d, tn), lambda j: (0, j)),
                  pl.BlockSpec((1, tn), lambda j: (0, j))],
        out_specs=pl.BlockSpec((bsz, tn), lambda j: (0, j)),
        out_shape=jax.ShapeDtypeStruct((bsz, n), F32),
        compiler_params=_params("parallel"),
        name="ada",
    )(c, w, b.reshape(1, n))


def _rope_kernel(cos_ref, sin_ref):
    rows = cos_ref.shape[0]
    pos = (lax.broadcasted_iota(jnp.int32, (rows, LANES), 0) + pl.program_id(0) * rows).astype(F32)
    lane = lax.broadcasted_iota(jnp.int32, (rows, LANES), 1)
    half = ROPE_DIM // 2
    inv_freq = jnp.exp((lane % half).astype(F32) * (-(2.0 / ROPE_DIM) * math.log(ROPE_THETA)))
    ang = pos * inv_freq
    cos_ref[...] = jnp.where(lane < ROPE_DIM, jnp.cos(ang), 1.0)
    sin = jnp.sin(ang)
    sin_ref[...] = jnp.where(lane < half, -sin, jnp.where(lane < ROPE_DIM, sin, 0.0))


def _rope_tables(seq):
    rows = 256
    return pl.pallas_call(
        _rope_kernel,
        grid=(seq // rows,),
        out_specs=[pl.BlockSpec((rows, LANES), lambda i: (i, 0))] * 2,
        out_shape=[jax.ShapeDtypeStruct((seq, LANES), F32)] * 2,
        compiler_params=_params("parallel"),
        name="rope",
    )()


def _inproj_kernel(x_ref, ada_ref, nw_ref, w_ref, o_ref, h_ref):
    @pl.when(pl.program_id(1) == 0)
    def _():
        x = x_ref[...]
        y = x * lax.rsqrt(jnp.mean(x * x, axis=-1, keepdims=True) + NORM_EPS) * nw_ref[...]
        h_ref[...] = (y * (1.0 + ada_ref[1:2, :]) + ada_ref[0:1, :]).astype(BF16)

    o_ref[...] = _dot(h_ref[...], w_ref[...])


def _inproj(x2, ada3, norm_w, w_bf16, seq):
    t, d = x2.shape
    n = w_bf16.shape[1]
    tm, tn = 1024, 1024
    return pl.pallas_call(
        _inproj_kernel,
        grid=(t // tm, n // tn),
        in_specs=[pl.BlockSpec((tm, d), lambda i, j: (i, 0)),
                  pl.BlockSpec((None, 6, d), lambda i, j: (i * tm // seq, 0, 0)),
                  pl.BlockSpec((1, d), lambda i, j: (0, 0)),
                  pl.BlockSpec((d, tn), lambda i, j: (0, j))],
        out_specs=pl.BlockSpec((tm, tn), lambda i, j: (i, j)),
        out_shape=jax.ShapeDtypeStruct((t, n), F32),
        scratch_shapes=[pltpu.VMEM((tm, d), BF16)],
        compiler_params=_params("parallel", "arbitrary"),
        name="in_proj",
    )(x2, ada3, norm_w, w_bf16)


def _attn_kernel(q1, k1, v1, q2, k2, v2, q3, k3, v3, cos_ref, sin_ref, o_ref, qs, ks, vs, acc_s, m_s, l_s):
    seq = q1.shape[0]
    nblk = seq // ATTN_BLOCK
    half = ROPE_DIM // 2
    pr = lax.broadcasted_iota(jnp.int32, (LANES, LANES), 0)
    pc = lax.broadcasted_iota(jnp.int32, (LANES, LANES), 1)
    perm = (((pr == pc + half) & (pc < half)) | ((pr == pc - half) & (pc >= half) & (pc < ROPE_DIM))).astype(BF16)
    perm2 = jnp.concatenate([perm, perm], axis=0)

    def rotary(t):
        hi, lo = _bf16_parts(t, 2)
        partner = _dot(jnp.concatenate([hi, lo], axis=1).astype(BF16), perm2)
        return t * cos_ref[...] + partner * sin_ref[...]

    qi = lax.broadcasted_iota(jnp.int32, (ATTN_BLOCK, 2 * ATTN_BLOCK), 0)
    kj = lax.broadcasted_iota(jnp.int32, (ATTN_BLOCK, 2 * ATTN_BLOCK), 1)
    dist = qi + ATTN_BLOCK - kj
    scale = HEAD_DIM ** -0.5

    for g, (refs, (window, dil)) in enumerate(zip(((q1, k1, v1), (q2, k2, v2), (q3, k3, v3)), ATTN_GROUPS)):
        q_ref, k_ref, v_ref = refs
        n_back = window // dil
        pad = ATTN_BLOCK * dil
        in_window = (dist >= 0) & (dist <= n_back)
        qs[...] = rotary(q_ref[...])
        ks[0:pad, :] = jnp.zeros((pad, LANES), F32)
        vs[0:pad, :] = jnp.zeros((pad, LANES), F32)
        ks[pad:pad + seq, :] = rotary(k_ref[...])
        vs[pad:pad + seq, :] = v_ref[...]

        def rows_of(idx, dil=dil, pad=pad):
            res = lax.rem(idx, dil)
            nb = idx // dil
            start = res + nb * pad
            if dil == 1:
                return nb, pl.ds(start, ATTN_BLOCK), pl.ds(start, 2 * ATTN_BLOCK)
            return nb, pl.ds(start, ATTN_BLOCK, stride=dil), pl.ds(start, 2 * ATTN_BLOCK, stride=dil)

        def blocks(i, carry, g=g, rows_of=rows_of, in_window=in_window):
            rows = [rows_of(i * ATTN_BPI + j) for j in range(ATTN_BPI)]
            ss = [_dot_nt(qs[rq, :], ks[rk, :]) * scale for _, rq, rk in rows]
            ps, ms = [], []
            for (nb, _, _), s in zip(rows, ss):
                s = jnp.where(in_window & ((nb > 0) | (kj >= ATTN_BLOCK)), s, -jnp.inf)
                m = jnp.max(s, axis=-1, keepdims=True)
                ms.append(m)
                ps.append(jnp.exp(s - m))
            accs = [_dot(p, vs[rk, :]) for (_, _, rk), p in zip(rows, ps)]
            for (_, rq, _), acc, m, p in zip(rows, accs, ms, ps):
                acc_s[g, rq, :] = acc
                m_s[g, rq, :] = jnp.broadcast_to(m, (ATTN_BLOCK, LANES))
                l_s[g, rq, :] = jnp.broadcast_to(jnp.sum(p, axis=-1, keepdims=True), (ATTN_BLOCK, LANES))
            return carry

        lax.fori_loop(0, nblk // ATTN_BPI, blocks, 0)

    def merge(i, carry):
        rows = pl.ds(pl.multiple_of(i * ATTN_BLOCK, ATTN_BLOCK), ATTN_BLOCK)
        ms = [m_s[g, rows, :] for g in range(3)]
        mx = jnp.maximum(jnp.maximum(ms[0], ms[1]), ms[2])
        ws = [jnp.exp(m - mx) for m in ms]
        num = ws[0] * acc_s[0, rows, :] + ws[1] * acc_s[1, rows, :] + ws[2] * acc_s[2, rows, :]
        den = ws[0] * l_s[0, rows, :] + ws[1] * l_s[1, rows, :] + ws[2] * l_s[2, rows, :]
        o_ref[rows, :] = (num / den).astype(o_ref.dtype)
        return carry

    lax.fori_loop(0, nblk, merge, 0)


def _attention(proj, cos_t, sin_t, bsz, seq):
    for window, dil in ATTN_GROUPS:
        assert window // dil == ATTN_BLOCK and seq % (ATTN_BLOCK * dil) == 0
    pad_max = ATTN_BLOCK * max(d for _, d in ATTN_GROUPS)

    def head_spec(col0, g):
        blk0 = col0 // HEAD_DIM + g * ATTN_HPG
        return pl.BlockSpec((seq, HEAD_DIM), lambda b, h: (b, blk0 + h))

    in_specs = []
    for g in range(len(ATTN_GROUPS)):
        in_specs += [head_spec(COL_AQ, g), head_spec(COL_AK, g), head_spec(COL_AV, g)]
    in_specs += [pl.BlockSpec((seq, LANES), lambda b, h: (0, 0))] * 2
    return pl.pallas_call(
        _attn_kernel,
        grid=(bsz, ATTN_HPG),
        in_specs=in_specs,
        out_specs=pl.BlockSpec((seq, HEAD_DIM), lambda b, h: (b, h)),
        out_shape=jax.ShapeDtypeStruct((bsz * seq, ATTN_HPG * HEAD_DIM), BF16),
        scratch_shapes=[pltpu.VMEM((seq, LANES), F32),
                        pltpu.VMEM((seq + pad_max, LANES), F32),
                        pltpu.VMEM((seq + pad_max, LANES), F32),
                        pltpu.VMEM((3, seq, LANES), F32),
                        pltpu.VMEM((3, seq, LANES), F32),
                        pltpu.VMEM((3, seq, LANES), F32)],
        compiler_params=_params("parallel", "parallel"),
        name="attention",
    )(*([proj] * 9), cos_t, sin_t)


def _bf16_parts(a, n):
    parts, rest = [], a
    for _ in range(n):
        piece = rest.astype(BF16).astype(F32)
        parts.append(piece)
        rest = rest - piece
    return parts


def _lhs3(a):
    hi, lo = _bf16_parts(a, 2)
    return jnp.concatenate([hi, hi, lo], axis=1).astype(BF16)


def _rhs3(b):
    hi, lo = _bf16_parts(b, 2)
    return jnp.concatenate([hi, lo, hi], axis=0).astype(BF16)


def _split3_rows(b):
    return jnp.concatenate(_bf16_parts(b, 3), axis=0).astype(BF16)


def _deltanet_kernel(q_ref, k_ref, v_ref, z_ref, ba_ref, cwq_ref, cwk_ref, cwv_ref, alog_ref, dtb_ref, nw_ref,
                     o_ref, kq_s, b_s, o_s, cd_s):
    seq = q_ref.shape[0]
    nchunk = seq // DN_CHUNK
    c64 = DN_CHUNK
    ri = lax.broadcasted_iota(jnp.int32, (c64, c64), 0)
    ci = lax.broadcasted_iota(jnp.int32, (c64, c64), 1)
    lower_incl = ri >= ci
    lower_strict = ri > ci
    ones_lower = lower_incl.astype(F32)
    eye = (ri == ci).astype(F32)
    lane = lax.broadcasted_iota(jnp.int32, (c64, LANES), 1)
    lane16 = lax.broadcasted_iota(jnp.int32, (1, DN_HEADS), 1)

    eye_bf16 = eye.astype(BF16)
    ones3 = jnp.concatenate([ones_lower] * 3, axis=1).astype(BF16)

    def conv_silu(x_ref, cw_ref, hh, r0, c):
        cols = slice(hh * DN_DIM, (hh + 1) * DN_DIM)
        prev = x_ref[pl.ds(pl.multiple_of(jnp.maximum(r0 - SUBLANES, 0), SUBLANES), SUBLANES), cols]
        prev = jnp.where(c > 0, prev, 0.0)
        win = jnp.concatenate([prev, x_ref[pl.ds(r0, c64), cols]], axis=0)
        y = win[SUBLANES:, :] * cw_ref[DN_CONV - 1:DN_CONV, cols]
        for j in range(1, DN_CONV):
            y = y + pltpu.roll(win, j, 0)[SUBLANES:, :] * cw_ref[DN_CONV - 1 - j:DN_CONV - j, cols]
        return _silu(y)

    def prepare(c, hh):
        r0 = pl.multiple_of(c * c64, c64)
        head = pl.program_id(1) * DN_HB + hh
        qx = conv_silu(q_ref, cwq_ref, hh, r0, c)
        kx = conv_silu(k_ref, cwk_ref, hh, r0, c)
        vx = conv_silu(v_ref, cwv_ref, hh, r0, c)
        qn = qx * lax.rsqrt(jnp.sum(qx * qx, axis=-1, keepdims=True) + 1e-6) * (DN_DIM ** -0.5)
        kn = kx * lax.rsqrt(jnp.sum(kx * kx, axis=-1, keepdims=True) + 1e-6)
        ba = ba_ref[pl.ds(r0, c64), :]
        b_col = jnp.sum(jnp.where(lane == head, ba, 0.0), axis=-1, keepdims=True)
        a_col = jnp.sum(jnp.where(lane == head + DN_HEADS, ba, 0.0), axis=-1, keepdims=True)
        a_log = jnp.sum(jnp.where(lane16 == head, alog_ref[...], 0.0), axis=-1, keepdims=True)
        dt_b = jnp.sum(jnp.where(lane16 == head, dtb_ref[...], 0.0), axis=-1, keepdims=True)
        beta = jax.nn.sigmoid(b_col)
        ax = a_col + dt_b
        softplus = jnp.maximum(ax, 0.0) + jnp.log1p(jnp.exp(-jnp.abs(ax)))
        g = -jnp.exp(a_log) * softplus
        gm = jnp.where(lower_strict, jnp.broadcast_to(g, (c64, c64)), 0.0)
        kb = kn * beta
        return dict(qn=qn, kn=kn, vb=vx * beta, kb=kb, g=g, gm3=_split3_rows(gm),
                    kbqn=jnp.concatenate([kb, qn], axis=0).astype(BF16), kn16=kn.astype(BF16))

    def local(i, carry):
        todo = [(i * DN_CPI + j, hh) for j in range(DN_CPI) for hh in range(DN_HB)]
        ps = [prepare(c, hh) for c, hh in todo]
        diffs = [_dot(ones3, p["gm3"]) for p in ps]
        scores = [_dot_nt(p["kbqn"], p["kn16"]) for p in ps]
        mids = []
        for p, diff, sc in zip(ps, diffs, scores):
            gam = diff[:, 0:1] + p["g"][0:1, :]
            gam_last = gam[c64 - 1:c64, :]
            decay = jnp.where(lower_incl, jnp.exp(jnp.where(lower_incl, diff, 0.0)), 0.0)
            eg = jnp.exp(gam)
            mids.append(dict(nmat=-jnp.where(lower_strict, sc[:c64] * decay, 0.0), qk=sc[c64:] * decay,
                             rhs=jnp.concatenate([p["vb"], p["kb"] * eg], axis=-1), qd=p["qn"] * eg,
                             kd=(p["kn"] * jnp.exp(gam_last - gam)).astype(BF16),
                             cd=jnp.broadcast_to(jnp.exp(gam_last), (SUBLANES, LANES))))
        invs = [eye + m["nmat"] for m in mids]
        pws = [_dot(_lhs3(m["nmat"]), _rhs3(m["nmat"])) for m in mids]
        for _ in range(4):
            boths = [_dot(_lhs3(pw), _rhs3(jnp.concatenate([inv, pw], axis=1))) for inv, pw in zip(invs, pws)]
            invs = [inv + b[:, :c64] for inv, b in zip(invs, boths)]
            pws = [b[:, c64:] for b in boths]
        ys = [_dot(_lhs3(inv), _rhs3(m["rhs"])) for inv, m in zip(invs, mids)]
        sols = [y + _dot(_lhs3(pw), _rhs3(y)) for y, pw in zip(ys, pws)]
        kdts = [_dot_tn(m["kd"], eye_bf16) for m in mids]
        prods = [_dot(jnp.concatenate([kd_t, m["qk"]], axis=0).astype(BF16),
                      jnp.concatenate([sol[:, DN_DIM:], sol[:, :DN_DIM]], axis=1).astype(BF16))
                 for m, sol, kd_t in zip(mids, sols, kdts)]
        for (c, hh), m, pr in zip(todo, mids, prods):
            kq_s[hh, pl.ds(pl.multiple_of(c * (DN_DIM + c64), c64), DN_DIM + c64), :] = jnp.concatenate(
                [pr[:DN_DIM, :DN_DIM], m["qd"] - pr[DN_DIM:, :DN_DIM]], axis=0).astype(BF16)
            b_s[hh, pl.ds(pl.multiple_of(c * DN_DIM, DN_DIM), DN_DIM), :] = pr[:DN_DIM, DN_DIM:]
            o_s[hh, pl.ds(pl.multiple_of(c * c64, c64), c64), :] = pr[DN_DIM:, DN_DIM:]
            cd_s[hh, pl.ds(pl.multiple_of(c * SUBLANES, SUBLANES), SUBLANES), :] = m["cd"]
        return carry

    lax.fori_loop(0, nchunk // DN_CPI, local, 0)

    def scan(c, states):
        r0 = pl.multiple_of(c * c64, c64)
        rows = pl.ds(r0, c64)
        heads = range(DN_HB)
        rkq = pl.ds(pl.multiple_of(c * (DN_DIM + c64), c64), DN_DIM + c64)
        rb = pl.ds(pl.multiple_of(c * DN_DIM, DN_DIM), DN_DIM)
        cds = [cd_s[hh, pl.ds(pl.multiple_of(c * SUBLANES, SUBLANES), SUBLANES), :][0:1, :] for hh in heads]
        rs = [_dot(kq_s[hh, rkq, :], states[hh].astype(BF16)) for hh in heads]
        new_states = [states[hh] * cds[hh] - rs[hh][:DN_DIM] + b_s[hh, rb, :] for hh in heads]
        outs = []
        for hh in heads:
            o = rs[hh][DN_DIM:] + o_s[hh, rows, :]
            on = o * lax.rsqrt(jnp.mean(o * o, axis=-1, keepdims=True) + NORM_EPS) * nw_ref[...]
            outs.append((on * _silu(z_ref[rows, hh * DN_DIM:(hh + 1) * DN_DIM])).astype(o_ref.dtype))
        for hh in heads:
            o_ref[rows, hh * DN_DIM:(hh + 1) * DN_DIM] = outs[hh]
        return tuple(new_states)

    lax.fori_loop(0, nchunk, scan, tuple(jnp.zeros((DN_DIM, DN_DIM), F32) for _ in range(DN_HB)))


def _deltanet(proj, conv_w, a_log, dt_bias, norm_w, bsz, seq):
    wb = DN_DIM * DN_HB
    hblocks = DN_HEADS // DN_HB
    nchunk = seq // DN_CHUNK
    assert nchunk % DN_CPI == 0

    def act_spec(col0):
        blk0 = col0 // wb
        return pl.BlockSpec((seq, wb), lambda b, h: (b, blk0 + h))

    def cw_spec(part):
        blk0 = part * DN_WIDTH // wb
        return pl.BlockSpec((DN_CONV, wb), lambda b, h: (0, blk0 + h))

    return pl.pallas_call(
        _deltanet_kernel,
        grid=(bsz, hblocks),
        in_specs=[act_spec(COL_DQ), act_spec(COL_DK), act_spec(COL_DV), act_spec(COL_DZ),
                  pl.BlockSpec((seq, LANES), lambda b, h: (b, COL_BA // LANES)),
                  cw_spec(0), cw_spec(1), cw_spec(2),
                  pl.BlockSpec((1, DN_HEADS), lambda b, h: (0, 0)),
                  pl.BlockSpec((1, DN_HEADS), lambda b, h: (0, 0)),
                  pl.BlockSpec((1, DN_DIM), lambda b, h: (0, 0))],
        out_specs=pl.BlockSpec((seq, wb), lambda b, h: (b, h)),
        out_shape=jax.ShapeDtypeStruct((bsz * seq, DN_WIDTH), BF16),
        scratch_shapes=[pltpu.VMEM((DN_HB, nchunk * (DN_DIM + DN_CHUNK), DN_DIM), BF16),
                        pltpu.VMEM((DN_HB, nchunk * DN_DIM, DN_DIM), F32),
                        pltpu.VMEM((DN_HB, seq, DN_DIM), F32),
                        pltpu.VMEM((DN_HB, nchunk * SUBLANES, LANES), F32)],
        compiler_params=_params("parallel", "parallel"),
        name="deltanet",
    )(proj, proj, proj, proj, proj, conv_w, conv_w, conv_w,
      a_log.reshape(1, DN_HEADS), dt_bias.reshape(1, DN_HEADS), norm_w.reshape(1, DN_DIM))


def _mixout_kernel(ya_ref, og_ref, ga_ref, gd_ref, x_ref, ada_ref, wa_ref, wd_ref, wm_ref, npost_ref, npre_ref,
                   xmid_ref, h2_ref):
    y_attn = _dot(ya_ref[...], wa_ref[...])
    y_delta = _dot(og_ref[...], wd_ref[...])
    merged = jax.nn.sigmoid(ga_ref[...]) * y_attn + jax.nn.sigmoid(gd_ref[...]) * y_delta
    y = _dot(merged.astype(BF16), wm_ref[...])
    yn = y * lax.rsqrt(jnp.mean(y * y, axis=-1, keepdims=True) + NORM_EPS) * npost_ref[...]
    x_mid = x_ref[...] + ada_ref[2:3, :] * yn
    xmid_ref[...] = x_mid
    hn = x_mid * lax.rsqrt(jnp.mean(x_mid * x_mid, axis=-1, keepdims=True) + NORM_EPS) * npre_ref[...]
    h2_ref[...] = hn * (1.0 + ada_ref[4:5, :]) + ada_ref[3:4, :]


def _mixout(y_attn, o_gated, proj, x2, ada3, w_attn, w_delta, w_mix, norm_post, norm_pre_ffn, seq):
    t, d = x2.shape
    tm = 256
    const = dict(pipeline_mode=pl.Buffered(1))
    return pl.pallas_call(
        _mixout_kernel,
        grid=(t // tm,),
        in_specs=[pl.BlockSpec((tm, y_attn.shape[1]), lambda i: (i, 0)),
                  pl.BlockSpec((tm, d), lambda i: (i, 0)),
                  pl.BlockSpec((tm, d), lambda i: (i, COL_GA // d)),
                  pl.BlockSpec((tm, d), lambda i: (i, COL_GD // d)),
                  pl.BlockSpec((tm, d), lambda i: (i, 0)),
                  pl.BlockSpec((None, 6, d), lambda i: (i * tm // seq, 0, 0)),
                  pl.BlockSpec(w_attn.shape, lambda i: (0, 0), **const),
                  pl.BlockSpec(w_delta.shape, lambda i: (0, 0), **const),
                  pl.BlockSpec(w_mix.shape, lambda i: (0, 0), **const),
                  pl.BlockSpec((1, d), lambda i: (0, 0)),
                  pl.BlockSpec((1, d), lambda i: (0, 0))],
        out_specs=[pl.BlockSpec((tm, d), lambda i: (i, 0))] * 2,
        out_shape=[jax.ShapeDtypeStruct((t, d), F32)] * 2,
        compiler_params=_params("parallel"),
        name="mix_out",
    )(y_attn, o_gated, proj, proj, x2, ada3, w_attn, w_delta, w_mix, norm_post, norm_pre_ffn)


def _topk_rows(s, k):
    n = s.shape[0]
    rid = lax.broadcasted_iota(jnp.int32, s.shape, 0)
    vals, ids = [], []
    for _ in range(k):
        m = jnp.max(s, axis=0, keepdims=True)
        sel = jnp.min(jnp.where(s == m, rid, n), axis=0, keepdims=True)
        vals.append(m)
        ids.append(sel)
        s = jnp.where(rid == sel, -jnp.inf, s)
    return jnp.concatenate(vals, axis=0), jnp.concatenate(ids, axis=0)


def _peer_route_kernel(h_ref, wq_ref, keys_ref, idx_ref, gw_ref, q_s, idx_s, gw_s):
    tm = h_ref.shape[0]
    hblock = pl.program_id(1)

    @pl.when(hblock == 0)
    def _():
        q = _dot(h_ref[...].astype(BF16), wq_ref[...])
        for hh in range(PEER_HEADS):
            q_s[hh] = q[:, hh * PEER_QDIM:(hh + 1) * PEER_QDIM]

    for hh in range(PEER_HPS):
        _route_head(hblock * PEER_HPS + hh, q_s, keys_ref.at[hh], idx_s, gw_s, tm)

    @pl.when(hblock == PEER_HEADS // PEER_HPS - 1)
    def _():
        for bb in range(tm // PEER_TB):
            tok = slice(bb * PEER_TB, (bb + 1) * PEER_TB)
            gw_ref[bb] = gw_s[:, tok]
            idx_ref[bb] = idx_s[:, tok].T


def _route_head(head, q_s, keys_ref, idx_s, gw_s, tm):
    half = PEER_QDIM // 2
    qh = q_s[head]
    tops = []
    for p in range(2):
        keys_hi, keys_lo = _bf16_parts(keys_ref[p], 2)
        q_hi, q_lo = _bf16_parts(qh[:, p * half:(p + 1) * half], 2)
        s = _dot_nt(jnp.concatenate([keys_hi, keys_hi, keys_lo], axis=1).astype(BF16),
                    jnp.concatenate([q_hi, q_lo, q_hi], axis=1).astype(BF16))
        tops.append(_topk_rows(s, PEER_TOPK))
    (s0, i0), (s1, i1) = tops
    counts = [PEER_TOPK // (a + 1) for a in range(PEER_TOPK)]
    n_cand = sum(counts)
    n_pad = -n_cand % SUBLANES
    cand = jnp.concatenate([s0[a:a + 1, :] + s1[0:counts[a], :] for a in range(PEER_TOPK)]
                           + [jnp.full((n_pad, tm), -jnp.inf, F32)], axis=0)
    best_s, best_row = _topk_rows(cand, PEER_TOPK)
    a_sel = jnp.zeros_like(best_row)
    b_sel = best_row
    start = 0
    for a in range(PEER_TOPK - 1):
        start += counts[a]
        past = (best_row >= start).astype(jnp.int32)
        a_sel = a_sel + past
        b_sel = b_sel - past * counts[a]
    e0 = jnp.zeros_like(best_row)
    e1 = jnp.zeros_like(best_row)
    for a in range(PEER_TOPK):
        e0 = e0 + jnp.where(a_sel == a, i0[a:a + 1, :], 0)
        e1 = e1 + jnp.where(b_sel == a, i1[a:a + 1, :], 0)
    ex = jnp.exp(best_s - jnp.max(best_s, axis=0, keepdims=True))
    rows = pl.ds(pl.multiple_of(head * PEER_TOPK, PEER_TOPK), PEER_TOPK)
    idx_s[rows, :] = e0 * PEER_KEYS + e1
    gw_s[rows, :] = ex / jnp.sum(ex, axis=0, keepdims=True)


def _peer_route(h2, w_query_bf16, keys):
    t, d = h2.shape
    tm = 256
    nb = tm // PEER_TB
    return pl.pallas_call(
        _peer_route_kernel,
        grid=(t // tm, PEER_HEADS // PEER_HPS),
        in_specs=[pl.BlockSpec((tm, d), lambda i, h: (i, 0)),
                  pl.BlockSpec(w_query_bf16.shape, lambda i, h: (0, 0)),
                  pl.BlockSpec((PEER_HPS, 2, PEER_KEYS, PEER_QDIM // 2), lambda i, h: (h, 0, 0, 0))],
        out_specs=[pl.BlockSpec((nb, PEER_TB, PEER_SLOTS), lambda i, h: (i, 0, 0)),
                   pl.BlockSpec((nb, PEER_SLOTS, PEER_TB), lambda i, h: (i, 0, 0))],
        out_shape=[jax.ShapeDtypeStruct((t // PEER_TB, PEER_TB, PEER_SLOTS), jnp.int32),
                   jax.ShapeDtypeStruct((t // PEER_TB, PEER_SLOTS, PEER_TB), F32)],
        scratch_shapes=[pltpu.VMEM((PEER_HEADS, tm, PEER_QDIM), F32),
                        pltpu.VMEM((PEER_SLOTS, tm), jnp.int32),
                        pltpu.VMEM((PEER_SLOTS, tm), F32)],
        compiler_params=_params("parallel", "arbitrary"),
        name="peer_route",
    )(h2, w_query_bf16, keys)


def _peer_gather_kernel(idx_hbm, gw_ref, h_ref, xmid_ref, ada_ref, nw_ref, tbl_hbm,
                        out_ref, buf, sem, idx_sm, idx_sem, ybuf):
    tb, d = h_ref.shape
    half = d // 2
    nsub = half // LANES
    slot_rows = PEER_SLOTS * PEER_PITCH
    nbuf = PEER_NBUF
    ngroups = tb // nbuf
    step = pl.program_id(0)
    lane = lax.broadcasted_iota(jnp.int32, (PEER_SLOTS, tb), 1)
    hi_mask = jnp.uint32(0xFFFF0000)

    def fetch_idx(g, par):
        return pltpu.make_async_copy(idx_hbm.at[step, pl.ds(g * nbuf, nbuf)], idx_sm.at[par], idx_sem.at[par])

    def issue(par, j):
        for k in range(PEER_SLOTS):
            dst = buf.at[pl.ds(j * slot_rows + k * PEER_PITCH, 2 * nsub)]
            pltpu.make_async_copy(tbl_hbm.at[idx_sm[par, j, k]], dst, sem.at[j]).start(priority=k % 2)

    def wait(j):
        rows = pl.ds(0, PEER_SLOTS * 2 * nsub)
        pltpu.make_async_copy(buf.at[rows], buf.at[rows], sem.at[j]).wait()

    def tile(j, tab, s):
        w = buf[pl.ds(j * slot_rows + tab * nsub + s, PEER_SLOTS, stride=PEER_PITCH), :]
        return pltpu.bitcast(w << 16, F32), pltpu.bitcast(w & hi_mask, F32)

    def compute(t, j):
        xrow = h_ref[pl.ds(t, 1), :]
        acc = None
        for s in range(nsub):
            lo, hi = tile(j, 0, s)
            term = lo * xrow[:, s * LANES:(s + 1) * LANES] + hi * xrow[:, half + s * LANES:half + (s + 1) * LANES]
            acc = term if acc is None else acc + term
        hcol = jnp.sum(acc, axis=-1, keepdims=True)
        gwcol = jnp.sum(jnp.where(lane == t, gw_ref[0], 0.0), axis=-1, keepdims=True)
        ccol = gwcol * _gelu_exact(hcol)
        los, his = [], []
        for s in range(nsub):
            lo, hi = tile(j, 1, s)
            los.append(jnp.sum(lo * ccol, axis=0, keepdims=True))
            his.append(jnp.sum(hi * ccol, axis=0, keepdims=True))
        ybuf[pl.ds(t, 1), :] = jnp.concatenate(los + his, axis=-1)

    def run_group(g, par_next, fetch_g):
        fetch_idx(g + 1, par_next).wait()
        if fetch_g is not None:
            fetch_idx(fetch_g, 1 - par_next).start()
        for j in range(nbuf):
            wait(j)
            compute(g * nbuf + j, j)
            issue(par_next, j)

    fetch_idx(0, 0).start()
    fetch_idx(0, 0).wait()
    fetch_idx(1, 1).start()
    for j in range(nbuf):
        issue(0, j)

    def pair(p, carry):
        run_group(2 * p, 1, 2 * p + 2)
        run_group(2 * p + 1, 0, 2 * p + 3)
        return carry

    lax.fori_loop(0, ngroups // 2 - 1, pair, 0)
    run_group(ngroups - 2, 1, None)
    for j in range(nbuf):
        wait(j)
        compute(tb - nbuf + j, j)
    y = ybuf[...]
    yn = y * lax.rsqrt(jnp.mean(y * y, axis=-1, keepdims=True) + NORM_EPS) * nw_ref[...]
    out_ref[...] = xmid_ref[...] + ada_ref[5:6, :] * yn


def _pack_tables_kernel(down_ref, up_ref, o_ref):
    half = down_ref.shape[1] // 2
    hi_mask = jnp.uint32(0xFFFF0000)

    def pack(x):
        lo = pltpu.bitcast(x[:, :half].astype(BF16).astype(F32), jnp.uint32)
        hi = pltpu.bitcast(x[:, half:].astype(BF16).astype(F32), jnp.uint32)
        return (lo >> 16) | (hi & hi_mask)

    o_ref[:, :half] = pack(down_ref[...])
    o_ref[:, half:] = pack(up_ref[...])


def _pack_expert_tables(down, up):
    rows, d = down.shape
    tr = 256
    packed = pl.pallas_call(
        _pack_tables_kernel,
        grid=(rows // tr,),
        in_specs=[pl.BlockSpec((tr, d), lambda i: (i, 0))] * 2,
        out_specs=pl.BlockSpec((tr, d), lambda i: (i, 0)),
        out_shape=jax.ShapeDtypeStruct((rows, d), jnp.uint32),
        compiler_params=_params("parallel"),
        name="pack_tables",
    )(down, up)
    return packed.reshape(rows, d // LANES, LANES)


def _peer_gather(idx_t, gw_t, h2, x_mid, ada3, norm_w, tables, seq):
    t, d = h2.shape
    blocks_per_batch = seq // PEER_TB
    nsub = d // 2 // LANES
    assert PEER_TB % (2 * PEER_NBUF) == 0 and 2 * nsub < PEER_PITCH
    return pl.pallas_call(
        _peer_gather_kernel,
        grid=(t // PEER_TB,),
        in_specs=[pl.BlockSpec(memory_space=pl.ANY),
                  pl.BlockSpec((1, PEER_SLOTS, PEER_TB), lambda i: (i, 0, 0)),
                  pl.BlockSpec((PEER_TB, d), lambda i: (i, 0)),
                  pl.BlockSpec((PEER_TB, d), lambda i: (i, 0)),
                  pl.BlockSpec((None, 6, d), lambda i: (i // blocks_per_batch, 0, 0)),
                  pl.BlockSpec((1, d), lambda i: (0, 0)),
                  pl.BlockSpec(memory_space=pl.ANY)],
        out_specs=pl.BlockSpec((PEER_TB, d), lambda i: (i, 0)),
        out_shape=jax.ShapeDtypeStruct((t, d), F32),
        scratch_shapes=[pltpu.VMEM((PEER_NBUF * PEER_SLOTS * PEER_PITCH, LANES), jnp.uint32),
                        pltpu.SemaphoreType.DMA((PEER_NBUF,)),
                        pltpu.SMEM((2, PEER_NBUF, PEER_SLOTS), jnp.int32),
                        pltpu.SemaphoreType.DMA((2,)),
                        pltpu.VMEM((PEER_TB, d), F32)],
        compiler_params=_params("arbitrary"),
        name="peer_gather",
    )(idx_t, gw_t, h2, x_mid, ada3, norm_w, tables)


def _permute_w_in(w_in):
    o_attn, o_dn, o_z = 0, 3 * ATTN_WIDTH, 3 * ATTN_WIDTH + 3 * DN_WIDTH
    o_b = o_z + DN_WIDTH
    o_gates = o_b + 2 * DN_HEADS
    parts = [w_in[:, o_gates:o_gates + 4096], w_in[:, o_dn:o_dn + 3 * DN_WIDTH], w_in[:, o_z:o_z + DN_WIDTH],
             w_in[:, o_attn:o_attn + 3 * ATTN_WIDTH], w_in[:, o_b:o_b + 2 * DN_HEADS]]
    w = jnp.concatenate(parts, axis=1)
    return jnp.pad(w, ((0, 0), (0, PROJ_WIDTH - w.shape[1]))).astype(BF16)


def _layer(x2, c, bsz, seq, w_ada, b_ada, norm_pre_mix, norm_post_mix, norm_pre_ffn, norm_post_ffn, w_in, conv_w,
           a_log, dt_bias, dn_norm_w, w_attn_out, w_delta_out, w_mix_out, peer_w_query, peer_sub_keys, peer_down,
           peer_up, cos_t, sin_t):
    d = x2.shape[1]
    row = lambda v: v.reshape(1, -1)
    ada3 = _ada(c, w_ada, b_ada).reshape(bsz, 6, d)
    proj = _inproj(x2, ada3, row(norm_pre_mix), _permute_w_in(w_in), seq)
    y_attn = _attention(proj, cos_t, sin_t, bsz, seq)
    o_gated = _deltanet(proj, conv_w, a_log, dt_bias, dn_norm_w, bsz, seq)
    x_mid, h2 = _mixout(y_attn, o_gated, proj, x2, ada3, w_attn_out.astype(BF16), w_delta_out.astype(BF16),
                        w_mix_out.astype(BF16), row(norm_post_mix), row(norm_pre_ffn), seq)
    idx_t, gw_t = _peer_route(h2, peer_w_query.astype(BF16), peer_sub_keys)
    return _peer_gather(idx_t, gw_t, h2, x_mid, ada3, row(norm_post_ffn), _pack_expert_tables(peer_down, peer_up),
                        seq)


def kernel(x, c, w_ada, b_ada, norm_pre_mix, norm_post_mix, norm_pre_ffn, norm_post_ffn, w_in, conv_w, a_log, dt_bias, dn_norm_w, w_attn_out, w_delta_out, w_mix_out, peer_w_query, peer_sub_keys, peer_down, peer_up):
    bsz, seq, d = x.shape
    x2 = x.reshape(bsz * seq, d)
    cos_t, sin_t = _rope_tables(seq)
    for layer in range(w_ada.shape[0]):
        x2 = _layer(x2, c, bsz, seq, w_ada[layer], b_ada[layer], norm_pre_mix[layer], norm_post_mix[layer],
                    norm_pre_ffn[layer], norm_post_ffn[layer], w_in[layer], conv_w[layer], a_log[layer],
                    dt_bias[layer], dn_norm_w[layer], w_attn_out[layer], w_delta_out[layer], w_mix_out[layer],
                    peer_w_query[layer], peer_sub_keys[layer], peer_down[layer], peer_up[layer], cos_t, sin_t)
    return x2.reshape(bsz, seq, d)
```

```python
import functools
import math

import jax
import jax.numpy as jnp
from jax import lax
from jax.experimental import pallas as pl
from jax.experimental.pallas import tpu as pltpu

F32 = jnp.float32
BF16 = jnp.bfloat16
HIGHEST = lax.Precision.HIGHEST
LANES = 128
SUBLANES = 8
VMEM_LIMIT = 56 * 1024 * 1024

NORM_EPS = 1e-6
HEAD_DIM = 128
ATTN_GROUPS = ((128, 1), (512, 4), (2048, 16))
ATTN_HPG = 4
ATTN_HEADS = ATTN_HPG * len(ATTN_GROUPS)
ATTN_WIDTH = ATTN_HEADS * HEAD_DIM
ATTN_BLOCK = 128
ATTN_BPI = 4
ROPE_THETA = 500000.0
ROPE_DIM = HEAD_DIM // 4
DN_HEADS = 16
DN_DIM = 128
DN_WIDTH = DN_HEADS * DN_DIM
DN_CONV = 4
DN_CHUNK = 64
DN_HB = 4
DN_CPI = 2
PEER_HEADS = 8
PEER_KEYS = 128
PEER_QDIM = 128
PEER_TOPK = 16
PEER_SLOTS = PEER_HEADS * PEER_TOPK
PEER_HPS = 8
PEER_TB = 128
PEER_NBUF = 8
PEER_PITCH = 17

COL_GA = 0
COL_GD = COL_GA + 2048
COL_DQ = COL_GD + 2048
COL_DK = COL_DQ + DN_WIDTH
COL_DV = COL_DK + DN_WIDTH
COL_DZ = COL_DV + DN_WIDTH
COL_AQ = COL_DZ + DN_WIDTH
COL_AK = COL_AQ + ATTN_WIDTH
COL_AV = COL_AK + ATTN_WIDTH
COL_BA = COL_AV + ATTN_WIDTH
PROJ_WIDTH = 17 * 1024


def _silu(x):
    return x * jax.nn.sigmoid(x)


def _gelu_exact(x):
    return 0.5 * x * (1.0 + lax.erf(x * (2.0 ** -0.5)))


def _dot(a, b, precision=None):
    return jnp.dot(a, b, preferred_element_type=F32, precision=precision)


def _dot_nt(a, b, precision=None):
    return lax.dot_general(a, b, (((1,), (1,)), ((), ())), preferred_element_type=F32, precision=precision)


def _dot_tn(a, b, precision=None):
    return lax.dot_general(a, b, (((0,), (0,)), ((), ())), preferred_element_type=F32, precision=precision)


def _params(*sem):
    return pltpu.CompilerParams(dimension_semantics=sem, vmem_limit_bytes=VMEM_LIMIT)


def _ada_kernel(c_ref, w_ref, b_ref, o_ref):
    o_ref[...] = _dot(_silu(c_ref[...]), w_ref[...], HIGHEST) + b_ref[...]


def _ada(c, w, b):
    bsz, d = c.shape
    n = w.shape[1]
    tn = 1536
    return pl.pallas_call(
        _ada_kernel,
        grid=(n // tn,),
        in_specs=[pl.BlockSpec((bsz, d), lambda j: (0, 0)),
                  pl.BlockSpec((d, tn), lambda j: (0, j)),
                  pl.BlockSpec((1, tn), lambda j: (0, j))],
        out_specs=pl.BlockSpec((bsz, tn), lambda j: (0, j)),
        out_shape=jax.ShapeDtypeStruct((bsz, n), F32),
        compiler_params=_params("parallel"),
        name="ada",
    )(c, w, b.reshape(1, n))


def _rope_kernel(cos_ref, sin_ref):
    rows = cos_ref.shape[0]
    pos = (lax.broadcasted_iota(jnp.int32, (rows, LANES), 0) + pl.program_id(0) * rows).astype(F32)
    lane = lax.broadcasted_iota(jnp.int32, (rows, LANES), 1)
    half = ROPE_DIM // 2
    inv_freq = jnp.exp((lane % half).astype(F32) * (-(2.0 / ROPE_DIM) * math.log(ROPE_THETA)))
    ang = pos * inv_freq
    cos_ref[...] = jnp.where(lane < ROPE_DIM, jnp.cos(ang), 1.0)
    sin = jnp.sin(ang)
    sin_ref[...] = jnp.where(lane < half, -sin, jnp.where(lane < ROPE_DIM, sin, 0.0))


def _rope_tables(seq):
    rows = 256
    return pl.pallas_call(
        _rope_kernel,
        grid=(seq // rows,),
        out_specs=[pl.BlockSpec((rows, LANES), lambda i: (i, 0))] * 2,
        out_shape=[jax.ShapeDtypeStruct((seq, LANES), F32)] * 2,
        compiler_params=_params("parallel"),
        name="rope",
    )()


def _inproj_kernel(x_ref, ada_ref, nw_ref, w_ref, o_ref, h_ref):
    @pl.when(pl.program_id(1) == 0)
    def _():
        x = x_ref[...]
        y = x * lax.rsqrt(jnp.mean(x * x, axis=-1, keepdims=True) + NORM_EPS) * nw_ref[...]
        h_ref[...] = (y * (1.0 + ada_ref[1:2, :]) + ada_ref[0:1, :]).astype(BF16)

    o_ref[...] = _dot(h_ref[...], w_ref[...])


def _inproj(x2, ada3, norm_w, w_bf16, seq):
    t, d = x2.shape
    n = w_bf16.shape[1]
    tm, tn = 1024, 1024
    return pl.pallas_call(
        _inproj_kernel,
        grid=(t // tm, n // tn),
        in_specs=[pl.BlockSpec((tm, d), lambda i, j: (i, 0)),
                  pl.BlockSpec((None, 6, d), lambda i, j: (i * tm // seq, 0, 0)),
                  pl.BlockSpec((1, d), lambda i, j: (0, 0)),
                  pl.BlockSpec((d, tn), lambda i, j: (0, j))],
        out_specs=pl.BlockSpec((tm, tn), lambda i, j: (i, j)),
        out_shape=jax.ShapeDtypeStruct((t, n), F32),
        scratch_shapes=[pltpu.VMEM((tm, d), BF16)],
        compiler_params=_params("parallel", "arbitrary"),
        name="in_proj",
    )(x2, ada3, norm_w, w_bf16)


def _attn_kernel(q1, k1, v1, q2, k2, v2, q3, k3, v3, cos_ref, sin_ref, o_ref, qs, ks, vs, acc_s, m_s, l_s):
    seq = q1.shape[0]
    nblk = seq // ATTN_BLOCK
    half = ROPE_DIM // 2
    pr = lax.broadcasted_iota(jnp.int32, (LANES, LANES), 0)
    pc = lax.broadcasted_iota(jnp.int32, (LANES, LANES), 1)
    perm = (((pr == pc + half) & (pc < half)) | ((pr == pc - half) & (pc >= half) & (pc < ROPE_DIM))).astype(BF16)
    perm2 = jnp.concatenate([perm, perm], axis=0)

    def rotary(t):
        hi, lo = _bf16_parts(t, 2)
        partner = _dot(jnp.concatenate([hi, lo], axis=1).astype(BF16), perm2)
        return t * cos_ref[...] + partner * sin_ref[...]

    qi = lax.broadcasted_iota(jnp.int32, (ATTN_BLOCK, 2 * ATTN_BLOCK), 0)
    kj = lax.broadcasted_iota(jnp.int32, (ATTN_BLOCK, 2 * ATTN_BLOCK), 1)
    dist = qi + ATTN_BLOCK - kj
    scale = HEAD_DIM ** -0.5

    for g, (refs, (window, dil)) in enumerate(zip(((q1, k1, v1), (q2, k2, v2), (q3, k3, v3)), ATTN_GROUPS)):
        q_ref, k_ref, v_ref = refs
        n_back = window // dil
        pad = ATTN_BLOCK * dil
        in_window = (dist >= 0) & (dist <= n_back)
        qs[...] = rotary(q_ref[...])
        ks[0:pad, :] = jnp.zeros((pad, LANES), F32)
        vs[0:pad, :] = jnp.zeros((pad, LANES), F32)
        ks[pad:pad + seq, :] = rotary(k_ref[...])
        vs[pad:pad + seq, :] = v_ref[...]

        def rows_of(idx, dil=dil, pad=pad):
            res = lax.rem(idx, dil)
            nb = idx // dil
            start = res + nb * pad
            if dil == 1:
                return nb, pl.ds(start, ATTN_BLOCK), pl.ds(start, 2 * ATTN_BLOCK)
            return nb, pl.ds(start, ATTN_BLOCK, stride=dil), pl.ds(start, 2 * ATTN_BLOCK, stride=dil)

        def blocks(i, carry, g=g, rows_of=rows_of, in_window=in_window):
            rows = [rows_of(i * ATTN_BPI + j) for j in range(ATTN_BPI)]
            ss = [_dot_nt(qs[rq, :], ks[rk, :]) * scale for _, rq, rk in rows]
            ps, ms = [], []
            for (nb, _, _), s in zip(rows, ss):
                s = jnp.where(in_window & ((nb > 0) | (kj >= ATTN_BLOCK)), s, -jnp.inf)
                m = jnp.max(s, axis=-1, keepdims=True)
                ms.append(m)
                ps.append(jnp.exp(s - m))
            accs = [_dot(p, vs[rk, :]) for (_, _, rk), p in zip(rows, ps)]
            for (_, rq, _), acc, m, p in zip(rows, accs, ms, ps):
                acc_s[g, rq, :] = acc
                m_s[g, rq, :] = jnp.broadcast_to(m, (ATTN_BLOCK, LANES))
                l_s[g, rq, :] = jnp.broadcast_to(jnp.sum(p, axis=-1, keepdims=True), (ATTN_BLOCK, LANES))
            return carry

        lax.fori_loop(0, nblk // ATTN_BPI, blocks, 0)

    def merge(i, carry):
        rows = pl.ds(pl.multiple_of(i * ATTN_BLOCK, ATTN_BLOCK), ATTN_BLOCK)
        ms = [m_s[g, rows, :] for g in range(3)]
        mx = jnp.maximum(jnp.maximum(ms[0], ms[1]), ms[2])
        ws = [jnp.exp(m - mx) for m in ms]
        num = ws[0] * acc_s[0, rows, :] + ws[1] * acc_s[1, rows, :] + ws[2] * acc_s[2, rows, :]
        den = ws[0] * l_s[0, rows, :] + ws[1] * l_s[1, rows, :] + ws[2] * l_s[2, rows, :]
        o_ref[rows, :] = (num / den).astype(o_ref.dtype)
        return carry

    lax.fori_loop(0, nblk, merge, 0)


def _attention(proj, cos_t, sin_t, bsz, seq):
    for window, dil in ATTN_GROUPS:
        assert window // dil == ATTN_BLOCK and seq % (ATTN_BLOCK * dil) == 0
    pad_max = ATTN_BLOCK * max(d for _, d in ATTN_GROUPS)

    def head_spec(col0, g):
        blk0 = col0 // HEAD_DIM + g * ATTN_HPG
        return pl.BlockSpec((seq, HEAD_DIM), lambda b, h: (b, blk0 + h))

    in_specs = []
    for g in range(len(ATTN_GROUPS)):
        in_specs += [head_spec(COL_AQ, g), head_spec(COL_AK, g), head_spec(COL_AV, g)]
    in_specs += [pl.BlockSpec((seq, LANES), lambda b, h: (0, 0))] * 2
    return pl.pallas_call(
        _attn_kernel,
        grid=(bsz, ATTN_HPG),
        in_specs=in_specs,
        out_specs=pl.BlockSpec((seq, HEAD_DIM), lambda b, h: (b, h)),
        out_shape=jax.ShapeDtypeStruct((bsz * seq, ATTN_HPG * HEAD_DIM), BF16),
        scratch_shapes=[pltpu.VMEM((seq, LANES), F32),
                        pltpu.VMEM((seq + pad_max, LANES), F32),
                        pltpu.VMEM((seq + pad_max, LANES), F32),
                        pltpu.VMEM((3, seq, LANES), F32),
                        pltpu.VMEM((3, seq, LANES), F32),
                        pltpu.VMEM((3, seq, LANES), F32)],
        compiler_params=_params("parallel", "parallel"),
        name="attention",
    )(*([proj] * 9), cos_t, sin_t)


def _bf16_parts(a, n):
    parts, rest = [], a
    for _ in range(n):
        piece = rest.astype(BF16).astype(F32)
        parts.append(piece)
        rest = rest - piece
    return parts


def _lhs3(a):
    hi, lo = _bf16_parts(a, 2)
    return jnp.concatenate([hi, hi, lo], axis=1).astype(BF16)


def _rhs3(b):
    hi, lo = _bf16_parts(b, 2)
    return jnp.concatenate([hi, lo, hi], axis=0).astype(BF16)


def _split3_rows(b):
    return jnp.concatenate(_bf16_parts(b, 3), axis=0).astype(BF16)


def _deltanet_kernel(q_ref, k_ref, v_ref, z_ref, ba_ref, cwq_ref, cwk_ref, cwv_ref, alog_ref, dtb_ref, nw_ref,
                     o_ref, kq_s, b_s, o_s, cd_s):
    seq = q_ref.shape[0]
    nchunk = seq // DN_CHUNK
    c64 = DN_CHUNK
    ri = lax.broadcasted_iota(jnp.int32, (c64, c64), 0)
    ci = lax.broadcasted_iota(jnp.int32, (c64, c64), 1)
    lower_incl = ri >= ci
    lower_strict = ri > ci
    ones_lower = lower_incl.astype(F32)
    eye = (ri == ci).astype(F32)
    lane = lax.broadcasted_iota(jnp.int32, (c64, LANES), 1)
    lane16 = lax.broadcasted_iota(jnp.int32, (1, DN_HEADS), 1)

    eye_bf16 = eye.astype(BF16)
    ones3 = jnp.concatenate([ones_lower] * 3, axis=1).astype(BF16)

    def conv_silu(x_ref, cw_ref, hh, r0, c):
        cols = slice(hh * DN_DIM, (hh + 1) * DN_DIM)
        prev = x_ref[pl.ds(pl.multiple_of(jnp.maximum(r0 - SUBLANES, 0), SUBLANES), SUBLANES), cols]
        prev = jnp.where(c > 0, prev, 0.0)
        win = jnp.concatenate([prev, x_ref[pl.ds(r0, c64), cols]], axis=0)
        y = win[SUBLANES:, :] * cw_ref[DN_CONV - 1:DN_CONV, cols]
        for j in range(1, DN_CONV):
            y = y + pltpu.roll(win, j, 0)[SUBLANES:, :] * cw_ref[DN_CONV - 1 - j:DN_CONV - j, cols]
        return _silu(y)

    def prepare(c, hh):
        r0 = pl.multiple_of(c * c64, c64)
        head = pl.program_id(1) * DN_HB + hh
        qx = conv_silu(q_ref, cwq_ref, hh, r0, c)
        kx = conv_silu(k_ref, cwk_ref, hh, r0, c)
        vx = conv_silu(v_ref, cwv_ref, hh, r0, c)
        qn = qx * lax.rsqrt(jnp.sum(qx * qx, axis=-1, keepdims=True) + 1e-6) * (DN_DIM ** -0.5)
        kn = kx * lax.rsqrt(jnp.sum(kx * kx, axis=-1, keepdims=True) + 1e-6)
        ba = ba_ref[pl.ds(r0, c64), :]
        b_col = jnp.sum(jnp.where(lane == head, ba, 0.0), axis=-1, keepdims=True)
        a_col = jnp.sum(jnp.where(lane == head + DN_HEADS, ba, 0.0), axis=-1, keepdims=True)
        a_log = jnp.sum(jnp.where(lane16 == head, alog_ref[...], 0.0), axis=-1, keepdims=True)
        dt_b = jnp.sum(jnp.where(lane16 == head, dtb_ref[...], 0.0), axis=-1, keepdims=True)
        beta = jax.nn.sigmoid(b_col)
        ax = a_col + dt_b
        softplus = jnp.maximum(ax, 0.0) + jnp.log1p(jnp.exp(-jnp.abs(ax)))
        g = -jnp.exp(a_log) * softplus
        gm = jnp.where(lower_strict, jnp.broadcast_to(g, (c64, c64)), 0.0)
        kb = kn * beta
        return dict(qn=qn, kn=kn, vb=vx * beta, kb=kb, g=g, gm3=_split3_rows(gm),
                    kbqn=jnp.concatenate([kb, qn], axis=0).astype(BF16), kn16=kn.astype(BF16))

    def local(i, carry):
        todo = [(i * DN_CPI + j, hh) for j in range(DN_CPI) for hh in range(DN_HB)]
        ps = [prepare(c, hh) for c, hh in todo]
        diffs = [_dot(ones3, p["gm3"]) for p in ps]
        scores = [_dot_nt(p["kbqn"], p["kn16"]) for p in ps]
        mids = []
        for p, diff, sc in zip(ps, diffs, scores):
            gam = diff[:, 0:1] + p["g"][0:1, :]
            gam_last = gam[c64 - 1:c64, :]
            decay = jnp.where(lower_incl, jnp.exp(jnp.where(lower_incl, diff, 0.0)), 0.0)
            eg = jnp.exp(gam)
            mids.append(dict(nmat=-jnp.where(lower_strict, sc[:c64] * decay, 0.0), qk=sc[c64:] * decay,
                             rhs=jnp.concatenate([p["vb"], p["kb"] * eg], axis=-1), qd=p["qn"] * eg,
                             kd=(p["kn"] * jnp.exp(gam_last - gam)).astype(BF16),
                             cd=jnp.broadcast_to(jnp.exp(gam_last), (SUBLANES, LANES))))
        invs = [eye + m["nmat"] for m in mids]
        pws = [_dot(_lhs3(m["nmat"]), _rhs3(m["nmat"])) for m in mids]
        for _ in range(4):
            boths = [_dot(_lhs3(pw), _rhs3(jnp.concatenate([inv, pw], axis=1))) for inv, pw in zip(invs, pws)]
            invs = [inv + b[:, :c64] for inv, b in zip(invs, boths)]
            pws = [b[:, c64:] for b in boths]
        ys = [_dot(_lhs3(inv), _rhs3(m["rhs"])) for inv, m in zip(invs, mids)]
        sols = [y + _dot(_lhs3(pw), _rhs3(y)) for y, pw in zip(ys, pws)]
        kdts = [_dot_tn(m["kd"], eye_bf16) for m in mids]
        prods = [_dot(jnp.concatenate([kd_t, m["qk"]], axis=0).astype(BF16),
                      jnp.concatenate([sol[:, DN_DIM:], sol[:, :DN_DIM]], axis=1).astype(BF16))
                 for m, sol, kd_t in zip(mids, sols, kdts)]
        for (c, hh), m, pr in zip(todo, mids, prods):
            kq_s[hh, pl.ds(pl.multiple_of(c * (DN_DIM + c64), c64), DN_DIM + c64), :] = jnp.concatenate(
                [pr[:DN_DIM, :DN_DIM], m["qd"] - pr[DN_DIM:, :DN_DIM]], axis=0).astype(BF16)
            b_s[hh, pl.ds(pl.multiple_of(c * DN_DIM, DN_DIM), DN_DIM), :] = pr[:DN_DIM, DN_DIM:].astype(BF16)
            o_s[hh, pl.ds(pl.multiple_of(c * c64, c64), c64), :] = pr[DN_DIM:, DN_DIM:].astype(BF16)
            cd_s[hh, pl.ds(pl.multiple_of(c * SUBLANES, SUBLANES), SUBLANES), :] = m["cd"]
        return carry

    lax.fori_loop(0, nchunk // DN_CPI, local, 0)

    def scan(c, states):
        r0 = pl.multiple_of(c * c64, c64)
        rows = pl.ds(r0, c64)
        heads = range(DN_HB)
        rkq = pl.ds(pl.multiple_of(c * (DN_DIM + c64), c64), DN_DIM + c64)
        rb = pl.ds(pl.multiple_of(c * DN_DIM, DN_DIM), DN_DIM)
        cds = [cd_s[hh, pl.ds(pl.multiple_of(c * SUBLANES, SUBLANES), SUBLANES), :][0:1, :] for hh in heads]
        rs = [_dot(kq_s[hh, rkq, :], states[hh].astype(BF16)) for hh in heads]
        new_states = [states[hh] * cds[hh] - rs[hh][:DN_DIM] + b_s[hh, rb, :] for hh in heads]
        outs = []
        for hh in heads:
            o = rs[hh][DN_DIM:] + o_s[hh, rows, :]
            on = o * lax.rsqrt(jnp.mean(o * o, axis=-1, keepdims=True) + NORM_EPS) * nw_ref[...]
            outs.append((on * _silu(z_ref[rows, hh * DN_DIM:(hh + 1) * DN_DIM])).astype(o_ref.dtype))
        for hh in heads:
            o_ref[rows, hh * DN_DIM:(hh + 1) * DN_DIM] = outs[hh]
        return tuple(new_states)

    lax.fori_loop(0, nchunk, scan, tuple(jnp.zeros((DN_DIM, DN_DIM), F32) for _ in range(DN_HB)))


def _deltanet(proj, conv_w, a_log, dt_bias, norm_w, bsz, seq):
    wb = DN_DIM * DN_HB
    hblocks = DN_HEADS // DN_HB
    nchunk = seq // DN_CHUNK
    assert nchunk % DN_CPI == 0

    def act_spec(col0):
        blk0 = col0 // wb
        return pl.BlockSpec((seq, wb), lambda b, h: (b, blk0 + h))

    def cw_spec(part):
        blk0 = part * DN_WIDTH // wb
        return pl.BlockSpec((DN_CONV, wb), lambda b, h: (0, blk0 + h))

    return pl.pallas_call(
        _deltanet_kernel,
        grid=(bsz, hblocks),
        in_specs=[act_spec(COL_DQ), act_spec(COL_DK), act_spec(COL_DV), act_spec(COL_DZ),
                  pl.BlockSpec((seq, LANES), lambda b, h: (b, COL_BA // LANES)),
                  cw_spec(0), cw_spec(1), cw_spec(2),
                  pl.BlockSpec((1, DN_HEADS), lambda b, h: (0, 0)),
                  pl.BlockSpec((1, DN_HEADS), lambda b, h: (0, 0)),
                  pl.BlockSpec((1, DN_DIM), lambda b, h: (0, 0))],
        out_specs=pl.BlockSpec((seq, wb), lambda b, h: (b, h)),
        out_shape=jax.ShapeDtypeStruct((bsz * seq, DN_WIDTH), BF16),
        scratch_shapes=[pltpu.VMEM((DN_HB, nchunk * (DN_DIM + DN_CHUNK), DN_DIM), BF16),
                        pltpu.VMEM((DN_HB, nchunk * DN_DIM, DN_DIM), BF16),
                        pltpu.VMEM((DN_HB, seq, DN_DIM), BF16),
                        pltpu.VMEM((DN_HB, nchunk * SUBLANES, LANES), F32)],
        compiler_params=_params("parallel", "parallel"),
        name="deltanet",
    )(proj, proj, proj, proj, proj, conv_w, conv_w, conv_w,
      a_log.reshape(1, DN_HEADS), dt_bias.reshape(1, DN_HEADS), norm_w.reshape(1, DN_DIM))


def _mixout_kernel(ya_ref, og_ref, ga_ref, gd_ref, x_ref, ada_ref, wa_ref, wd_ref, wm_ref, npost_ref, npre_ref,
                   xmid_ref, h2_ref):
    y_attn = _dot(ya_ref[...], wa_ref[...])
    y_delta = _dot(og_ref[...], wd_ref[...])
    merged = jax.nn.sigmoid(ga_ref[...]) * y_attn + jax.nn.sigmoid(gd_ref[...]) * y_delta
    y = _dot(merged.astype(BF16), wm_ref[...])
    yn = y * lax.rsqrt(jnp.mean(y * y, axis=-1, keepdims=True) + NORM_EPS) * npost_ref[...]
    x_mid = x_ref[...] + ada_ref[2:3, :] * yn
    xmid_ref[...] = x_mid
    hn = x_mid * lax.rsqrt(jnp.mean(x_mid * x_mid, axis=-1, keepdims=True) + NORM_EPS) * npre_ref[...]
    h2_ref[...] = hn * (1.0 + ada_ref[4:5, :]) + ada_ref[3:4, :]


def _mixout(y_attn, o_gated, proj, x2, ada3, w_attn, w_delta, w_mix, norm_post, norm_pre_ffn, seq):
    t, d = x2.shape
    tm = 256
    const = dict(pipeline_mode=pl.Buffered(1))
    return pl.pallas_call(
        _mixout_kernel,
        grid=(t // tm,),
        in_specs=[pl.BlockSpec((tm, y_attn.shape[1]), lambda i: (i, 0)),
                  pl.BlockSpec((tm, d), lambda i: (i, 0)),
                  pl.BlockSpec((tm, d), lambda i: (i, COL_GA // d)),
                  pl.BlockSpec((tm, d), lambda i: (i, COL_GD // d)),
                  pl.BlockSpec((tm, d), lambda i: (i, 0)),
                  pl.BlockSpec((None, 6, d), lambda i: (i * tm // seq, 0, 0)),
                  pl.BlockSpec(w_attn.shape, lambda i: (0, 0), **const),
                  pl.BlockSpec(w_delta.shape, lambda i: (0, 0), **const),
                  pl.BlockSpec(w_mix.shape, lambda i: (0, 0), **const),
                  pl.BlockSpec((1, d), lambda i: (0, 0)),
                  pl.BlockSpec((1, d), lambda i: (0, 0))],
        out_specs=[pl.BlockSpec((tm, d), lambda i: (i, 0))] * 2,
        out_shape=[jax.ShapeDtypeStruct((t, d), F32)] * 2,
        compiler_params=_params("parallel"),
        name="mix_out",
    )(y_attn, o_gated, proj, proj, x2, ada3, w_attn, w_delta, w_mix, norm_post, norm_pre_ffn)


def _topk_rows(s, k):
    n = s.shape[0]
    rid = lax.broadcasted_iota(jnp.int32, s.shape, 0)
    vals, ids = [], []
    for _ in range(k):
        m = jnp.max(s, axis=0, keepdims=True)
        sel = jnp.min(jnp.where(s == m, rid, n), axis=0, keepdims=True)
        vals.append(m)
        ids.append(sel)
        s = jnp.where(rid == sel, -jnp.inf, s)
    return jnp.concatenate(vals, axis=0), jnp.concatenate(ids, axis=0)


def _peer_route_kernel(h_ref, wq_ref, keys_ref, idx_ref, gw_ref, q_s, idx_s, gw_s):
    tm = h_ref.shape[0]
    hblock = pl.program_id(1)

    @pl.when(hblock == 0)
    def _():
        q = _dot(h_ref[...].astype(BF16), wq_ref[...])
        for hh in range(PEER_HEADS):
            q_s[hh] = q[:, hh * PEER_QDIM:(hh + 1) * PEER_QDIM]

    for hh in range(PEER_HPS):
        _route_head(hblock * PEER_HPS + hh, q_s, keys_ref.at[hh], idx_s, gw_s, tm)

    @pl.when(hblock == PEER_HEADS // PEER_HPS - 1)
    def _():
        for bb in range(tm // PEER_TB):
            tok = slice(bb * PEER_TB, (bb + 1) * PEER_TB)
            gw_ref[bb] = gw_s[:, tok]
            idx_ref[bb] = idx_s[:, tok].T


def _route_head(head, q_s, keys_ref, idx_s, gw_s, tm):
    half = PEER_QDIM // 2
    qh = q_s[head]
    tops = []
    for p in range(2):
        keys_hi, keys_lo = _bf16_parts(keys_ref[p], 2)
        q_hi, q_lo = _bf16_parts(qh[:, p * half:(p + 1) * half], 2)
        s = _dot_nt(jnp.concatenate([keys_hi, keys_hi, keys_lo], axis=1).astype(BF16),
                    jnp.concatenate([q_hi, q_lo, q_hi], axis=1).astype(BF16))
        tops.append(_topk_rows(s, PEER_TOPK))
    (s0, i0), (s1, i1) = tops
    counts = [PEER_TOPK // (a + 1) for a in range(PEER_TOPK)]
    n_cand = sum(counts)
    n_pad = -n_cand % SUBLANES
    cand = jnp.concatenate([s0[a:a + 1, :] + s1[0:counts[a], :] for a in range(PEER_TOPK)]
                           + [jnp.full((n_pad, tm), -jnp.inf, F32)], axis=0)
    best_s, best_row = _topk_rows(cand, PEER_TOPK)
    a_sel = jnp.zeros_like(best_row)
    b_sel = best_row
    start = 0
    for a in range(PEER_TOPK - 1):
        start += counts[a]
        past = (best_row >= start).astype(jnp.int32)
        a_sel = a_sel + past
        b_sel = b_sel - past * counts[a]
    e0 = jnp.zeros_like(best_row)
    e1 = jnp.zeros_like(best_row)
    for a in range(PEER_TOPK):
        e0 = e0 + jnp.where(a_sel == a, i0[a:a + 1, :], 0)
        e1 = e1 + jnp.where(b_sel == a, i1[a:a + 1, :], 0)
    ex = jnp.exp(best_s - jnp.max(best_s, axis=0, keepdims=True))
    rows = pl.ds(pl.multiple_of(head * PEER_TOPK, PEER_TOPK), PEER_TOPK)
    idx_s[rows, :] = e0 * PEER_KEYS + e1
    gw_s[rows, :] = ex / jnp.sum(ex, axis=0, keepdims=True)


def _peer_route(h2, w_query_bf16, keys):
    t, d = h2.shape
    tm = 256
    nb = tm // PEER_TB
    return pl.pallas_call(
        _peer_route_kernel,
        grid=(t // tm, PEER_HEADS // PEER_HPS),
        in_specs=[pl.BlockSpec((tm, d), lambda i, h: (i, 0)),
                  pl.BlockSpec(w_query_bf16.shape, lambda i, h: (0, 0)),
                  pl.BlockSpec((PEER_HPS, 2, PEER_KEYS, PEER_QDIM // 2), lambda i, h: (h, 0, 0, 0))],
        out_specs=[pl.BlockSpec((nb, PEER_TB, PEER_SLOTS), lambda i, h: (i, 0, 0)),
                   pl.BlockSpec((nb, PEER_SLOTS, PEER_TB), lambda i, h: (i, 0, 0))],
        out_shape=[jax.ShapeDtypeStruct((t // PEER_TB, PEER_TB, PEER_SLOTS), jnp.int32),
                   jax.ShapeDtypeStruct((t // PEER_TB, PEER_SLOTS, PEER_TB), F32)],
        scratch_shapes=[pltpu.VMEM((PEER_HEADS, tm, PEER_QDIM), F32),
                        pltpu.VMEM((PEER_SLOTS, tm), jnp.int32),
                        pltpu.VMEM((PEER_SLOTS, tm), F32)],
        compiler_params=_params("parallel", "arbitrary"),
        name="peer_route",
    )(h2, w_query_bf16, keys)


def _peer_gather_kernel(idx_hbm, gw_ref, h_ref, xmid_ref, ada_ref, nw_ref, tbl_hbm,
                        out_ref, buf, sem, idx_sm, idx_sem, ybuf):
    tb, d = h_ref.shape
    half = d // 2
    nsub = half // LANES
    slot_rows = PEER_SLOTS * PEER_PITCH
    nbuf = PEER_NBUF
    ngroups = tb // nbuf
    step = pl.program_id(0)
    lane = lax.broadcasted_iota(jnp.int32, (PEER_SLOTS, tb), 1)
    hi_mask = jnp.uint32(0xFFFF0000)

    def fetch_idx(g, par):
        return pltpu.make_async_copy(idx_hbm.at[step, pl.ds(g * nbuf, nbuf)], idx_sm.at[par], idx_sem.at[par])

    def issue(par, j):
        for k in range(PEER_SLOTS):
            dst = buf.at[pl.ds(j * slot_rows + k * PEER_PITCH, 2 * nsub)]
            pltpu.make_async_copy(tbl_hbm.at[idx_sm[par, j, k]], dst, sem.at[j]).start(priority=k % 2)

    def wait(j):
        rows = pl.ds(0, PEER_SLOTS * 2 * nsub)
        pltpu.make_async_copy(buf.at[rows], buf.at[rows], sem.at[j]).wait()

    def tile(j, tab, s):
        w = buf[pl.ds(j * slot_rows + tab * nsub + s, PEER_SLOTS, stride=PEER_PITCH), :]
        return pltpu.bitcast(w << 16, F32), pltpu.bitcast(w & hi_mask, F32)

    def compute(t, j):
        xrow = h_ref[pl.ds(t, 1), :]
        acc = None
        for s in range(nsub):
            lo, hi = tile(j, 0, s)
            term = lo * xrow[:, s * LANES:(s + 1) * LANES] + hi * xrow[:, half + s * LANES:half + (s + 1) * LANES]
            acc = term if acc is None else acc + term
        hcol = jnp.sum(acc, axis=-1, keepdims=True)
        gwcol = jnp.sum(jnp.where(lane == t, gw_ref[0], 0.0), axis=-1, keepdims=True)
        ccol = gwcol * _gelu_exact(hcol)
        los, his = [], []
        for s in range(nsub):
            lo, hi = tile(j, 1, s)
            los.append(jnp.sum(lo * ccol, axis=0, keepdims=True))
            his.append(jnp.sum(hi * ccol, axis=0, keepdims=True))
        ybuf[pl.ds(t, 1), :] = jnp.concatenate(los + his, axis=-1)

    def run_group(g, par_next, fetch_g):
        fetch_idx(g + 1, par_next).wait()
        if fetch_g is not None:
            fetch_idx(fetch_g, 1 - par_next).start()
        for j in range(nbuf):
            wait(j)
            compute(g * nbuf + j, j)
            issue(par_next, j)

    fetch_idx(0, 0).start()
    fetch_idx(0, 0).wait()
    fetch_idx(1, 1).start()
    for j in range(nbuf):
        issue(0, j)

    def pair(p, carry):
        run_group(2 * p, 1, 2 * p + 2)
        run_group(2 * p + 1, 0, 2 * p + 3)
        return carry

    lax.fori_loop(0, ngroups // 2 - 1, pair, 0)
    run_group(ngroups - 2, 1, None)
    for j in range(nbuf):
        wait(j)
        compute(tb - nbuf + j, j)
    y = ybuf[...]
    yn = y * lax.rsqrt(jnp.mean(y * y, axis=-1, keepdims=True) + NORM_EPS) * nw_ref[...]
    out_ref[...] = xmid_ref[...] + ada_ref[5:6, :] * yn


def _pack_tables_kernel(down_ref, up_ref, o_ref):
    half = down_ref.shape[1] // 2
    hi_mask = jnp.uint32(0xFFFF0000)

    def pack(x):
        lo = pltpu.bitcast(x[:, :half].astype(BF16).astype(F32), jnp.uint32)
        hi = pltpu.bitcast(x[:, half:].astype(BF16).astype(F32), jnp.uint32)
        return (lo >> 16) | (hi & hi_mask)

    o_ref[:, :half] = pack(down_ref[...])
    o_ref[:, half:] = pack(up_ref[...])


def _pack_expert_tables(down, up):
    rows, d = down.shape
    tr = 256
    packed = pl.pallas_call(
        _pack_tables_kernel,
        grid=(rows // tr,),
        in_specs=[pl.BlockSpec((tr, d), lambda i: (i, 0))] * 2,
        out_specs=pl.BlockSpec((tr, d), lambda i: (i, 0)),
        out_shape=jax.ShapeDtypeStruct((rows, d), jnp.uint32),
        compiler_params=_params("parallel"),
        name="pack_tables",
    )(down, up)
    return packed.reshape(rows, d // LANES, LANES)


def _peer_gather(idx_t, gw_t, h2, x_mid, ada3, norm_w, tables, seq):
    t, d = h2.shape
    blocks_per_batch = seq // PEER_TB
    nsub = d // 2 // LANES
    assert PEER_TB % (2 * PEER_NBUF) == 0 and 2 * nsub < PEER_PITCH
    return pl.pallas_call(
        _peer_gather_kernel,
        grid=(t // PEER_TB,),
        in_specs=[pl.BlockSpec(memory_space=pl.ANY),
                  pl.BlockSpec((1, PEER_SLOTS, PEER_TB), lambda i: (i, 0, 0)),
                  pl.BlockSpec((PEER_TB, d), lambda i: (i, 0)),
                  pl.BlockSpec((PEER_TB, d), lambda i: (i, 0)),
                  pl.BlockSpec((None, 6, d), lambda i: (i // blocks_per_batch, 0, 0)),
                  pl.BlockSpec((1, d), lambda i: (0, 0)),
                  pl.BlockSpec(memory_space=pl.ANY)],
        out_specs=pl.BlockSpec((PEER_TB, d), lambda i: (i, 0)),
        out_shape=jax.ShapeDtypeStruct((t, d), F32),
        scratch_shapes=[pltpu.VMEM((PEER_NBUF * PEER_SLOTS * PEER_PITCH, LANES), jnp.uint32),
                        pltpu.SemaphoreType.DMA((PEER_NBUF,)),
                        pltpu.SMEM((2, PEER_NBUF, PEER_SLOTS), jnp.int32),
                        pltpu.SemaphoreType.DMA((2,)),
                        pltpu.VMEM((PEER_TB, d), F32)],
        compiler_params=_params("arbitrary"),
        name="peer_gather",
    )(idx_t, gw_t, h2, x_mid, ada3, norm_w, tables)


def _permute_w_in(w_in):
    o_attn, o_dn, o_z = 0, 3 * ATTN_WIDTH, 3 * ATTN_WIDTH + 3 * DN_WIDTH
    o_b = o_z + DN_WIDTH
    o_gates = o_b + 2 * DN_HEADS
    parts = [w_in[:, o_gates:o_gates + 4096], w_in[:, o_dn:o_dn + 3 * DN_WIDTH], w_in[:, o_z:o_z + DN_WIDTH],
             w_in[:, o_attn:o_attn + 3 * ATTN_WIDTH], w_in[:, o_b:o_b + 2 * DN_HEADS]]
    w = jnp.concatenate(parts, axis=1)
    return jnp.pad(w, ((0, 0), (0, PROJ_WIDTH - w.shape[1]))).astype(BF16)


def _layer(x2, c, bsz, seq, w_ada, b_ada, norm_pre_mix, norm_post_mix, norm_pre_ffn, norm_post_ffn, w_in, conv_w,
           a_log, dt_bias, dn_norm_w, w_attn_out, w_delta_out, w_mix_out, peer_w_query, peer_sub_keys, peer_down,
           peer_up, cos_t, sin_t):
    d = x2.shape[1]
    row = lambda v: v.reshape(1, -1)
    ada3 = _ada(c, w_ada, b_ada).reshape(bsz, 6, d)
    proj = _inproj(x2, ada3, row(norm_pre_mix), _permute_w_in(w_in), seq)
    y_attn = _attention(proj, cos_t, sin_t, bsz, seq)
    o_gated = _deltanet(proj, conv_w, a_log, dt_bias, dn_norm_w, bsz, seq)
    x_mid, h2 = _mixout(y_attn, o_gated, proj, x2, ada3, w_attn_out.astype(BF16), w_delta_out.astype(BF16),
                        w_mix_out.astype(BF16), row(norm_post_mix), row(norm_pre_ffn), seq)
    idx_t, gw_t = _peer_route(h2, peer_w_query.astype(BF16), peer_sub_keys)
    return _peer_gather(idx_t, gw_t, h2, x_mid, ada3, row(norm_post_ffn), _pack_expert_tables(peer_down, peer_up),
                        seq)


def kernel(x, c, w_ada, b_ada, norm_pre_mix, norm_post_mix, norm_pre_ffn, norm_post_ffn, w_in, conv_w, a_log, dt_bias, dn_norm_w, w_attn_out, w_delta_out, w_mix_out, peer_w_query, peer_sub_keys, peer_down, peer_up):
    bsz, seq, d = x.shape
    x2 = x.reshape(bsz * seq, d)
    cos_t, sin_t = _rope_tables(seq)
    for layer in range(w_ada.shape[0]):
        x2 = _layer(x2, c, bsz, seq, w_ada[layer], b_ada[layer], norm_pre_mix[layer], norm_post_mix[layer],
                    norm_pre_ffn[layer], norm_post_ffn[layer], w_in[layer], conv_w[layer], a_log[layer],
                    dt_bias[layer], dn_norm_w[layer], w_attn_out[layer], w_delta_out[layer], w_mix_out[layer],
                    peer_w_query[layer], peer_sub_keys[layer], peer_down[layer], peer_up[layer], cos_t, sin_t)
    return x2.reshape(bsz, seq, d)
```

```python
import math

import jax
import jax.numpy as jnp
from jax import lax
from jax.experimental import pallas as pl
from jax.experimental.pallas import tpu as pltpu

F32 = jnp.float32
BF16 = jnp.bfloat16
HIGHEST = lax.Precision.HIGHEST
LANES = 128
SUBLANES = 8
VMEM_LIMIT = 56 * 1024 * 1024

NORM_EPS = 1e-6
HEAD_DIM = 128
ATTN_GROUPS = ((128, 1), (512, 4), (2048, 16))
ATTN_HPG = 4
ATTN_HEADS = ATTN_HPG * len(ATTN_GROUPS)
ATTN_WIDTH = ATTN_HEADS * HEAD_DIM
ATTN_BLOCK = 128
ATTN_BPI = 8
ROPE_THETA = 500000.0
ROPE_DIM = HEAD_DIM // 4
DN_HEADS = 16
DN_DIM = 128
DN_WIDTH = DN_HEADS * DN_DIM
DN_CONV = 4
DN_CHUNK = 64
DN_HB = 4
DN_CPI = 2
PEER_HEADS = 8
PEER_KEYS = 128
PEER_QDIM = 128
PEER_TOPK = 16
PEER_SLOTS = PEER_HEADS * PEER_TOPK
PEER_HPS = 8
PEER_TB = 128
PEER_NBUF = 8
PEER_PITCH = 17

COL_GA = 0
COL_GD = COL_GA + 2048
COL_DQ = COL_GD + 2048
COL_DK = COL_DQ + DN_WIDTH
COL_DV = COL_DK + DN_WIDTH
COL_DZ = COL_DV + DN_WIDTH
COL_AQ = COL_DZ + DN_WIDTH
COL_AK = COL_AQ + ATTN_WIDTH
COL_AV = COL_AK + ATTN_WIDTH
COL_BA = COL_AV + ATTN_WIDTH
PROJ_WIDTH = 17 * 1024


def _silu(x):
    return x * jax.nn.sigmoid(x)


def _gelu_exact(x):
    return 0.5 * x * (1.0 + lax.erf(x * (2.0 ** -0.5)))


def _dot(a, b, precision=None):
    return jnp.dot(a, b, preferred_element_type=F32, precision=precision)


def _dot_nt(a, b, precision=None):
    return lax.dot_general(a, b, (((1,), (1,)), ((), ())), preferred_element_type=F32, precision=precision)


def _dot_tn(a, b, precision=None):
    return lax.dot_general(a, b, (((0,), (0,)), ((), ())), preferred_element_type=F32, precision=precision)


def _params(*sem):
    return pltpu.CompilerParams(dimension_semantics=sem, vmem_limit_bytes=VMEM_LIMIT)


def _ada_kernel(c_ref, w_ref, b_ref, o_ref):
    o_ref[...] = _dot(_silu(c_ref[...]), w_ref[...], HIGHEST) + b_ref[...]


def _ada(c, w, b):
    bsz, d = c.shape
    n = w.shape[1]
    tn = 1536
    return pl.pallas_call(
        _ada_kernel,
        grid=(n // tn,),
        in_specs=[pl.BlockSpec((bsz, d), lambda j: (0, 0)),
                  pl.BlockSpec((d, tn), lambda j: (0, j)),
                  pl.BlockSpec((1, tn), lambda j: (0, j))],
        out_specs=pl.BlockSpec((bsz, tn), lambda j: (0, j)),
        out_shape=jax.ShapeDtypeStruct((bsz, n), F32),
        compiler_params=_params("parallel"),
        name="ada",
    )(c, w, b.reshape(1, n))


def _rope_kernel(cos_ref, sin_ref):
    rows = cos_ref.shape[0]
    pos = (lax.broadcasted_iota(jnp.int32, (rows, LANES), 0) + pl.program_id(0) * rows).astype(F32)
    lane = lax.broadcasted_iota(jnp.int32, (rows, LANES), 1)
    half = ROPE_DIM // 2
    inv_freq = jnp.exp((lane % half).astype(F32) * (-(2.0 / ROPE_DIM) * math.log(ROPE_THETA)))
    ang = pos * inv_freq
    cos_ref[...] = jnp.where(lane < ROPE_DIM, jnp.cos(ang), 1.0)
    sin = jnp.sin(ang)
    sin_ref[...] = jnp.where(lane < half, -sin, jnp.where(lane < ROPE_DIM, sin, 0.0))


def _rope_tables(seq):
    rows = 256
    return pl.pallas_call(
        _rope_kernel,
        grid=(seq // rows,),
        out_specs=[pl.BlockSpec((rows, LANES), lambda i: (i, 0))] * 2,
        out_shape=[jax.ShapeDtypeStruct((seq, LANES), F32)] * 2,
        compiler_params=_params("parallel"),
        name="rope",
    )()


def _inproj_kernel(x_ref, ada_ref, nw_ref, w_ref, o_ref, h_ref):
    @pl.when(pl.program_id(1) == 0)
    def _():
        x = x_ref[...]
        y = x * lax.rsqrt(jnp.mean(x * x, axis=-1, keepdims=True) + NORM_EPS) * nw_ref[...]
        h_ref[...] = (y * (1.0 + ada_ref[1:2, :]) + ada_ref[0:1, :]).astype(BF16)

    o_ref[...] = _dot(h_ref[...], w_ref[...])


def _inproj(x2, ada3, norm_w, w_bf16, seq):
    t, d = x2.shape
    n = w_bf16.shape[1]
    tm, tn = 1024, 1024
    return pl.pallas_call(
        _inproj_kernel,
        grid=(t // tm, n // tn),
        in_specs=[pl.BlockSpec((tm, d), lambda i, j: (i, 0)),
                  pl.BlockSpec((None, 6, d), lambda i, j: (i * tm // seq, 0, 0)),
                  pl.BlockSpec((1, d), lambda i, j: (0, 0)),
                  pl.BlockSpec((d, tn), lambda i, j: (0, j))],
        out_specs=pl.BlockSpec((tm, tn), lambda i, j: (i, j)),
        out_shape=jax.ShapeDtypeStruct((t, n), F32),
        scratch_shapes=[pltpu.VMEM((tm, d), BF16)],
        compiler_params=_params("parallel", "arbitrary"),
        name="in_proj",
    )(x2, ada3, norm_w, w_bf16)


def _attn_kernel(q1, k1, v1, q2, k2, v2, q3, k3, v3, cos_ref, sin_ref, o_ref, qs, ks, vs, acc_s, m_s, l_s):
    seq = q1.shape[0]
    nblk = seq // ATTN_BLOCK
    half = ROPE_DIM // 2
    pr = lax.broadcasted_iota(jnp.int32, (LANES, LANES), 0)
    pc = lax.broadcasted_iota(jnp.int32, (LANES, LANES), 1)
    perm = (((pr == pc + half) & (pc < half)) | ((pr == pc - half) & (pc >= half) & (pc < ROPE_DIM))).astype(BF16)
    perm2 = jnp.concatenate([perm, perm], axis=0)

    def rotary(t):
        hi, lo = _bf16_parts(t, 2)
        partner = _dot(jnp.concatenate([hi, lo], axis=1).astype(BF16), perm2)
        return t * cos_ref[...] + partner * sin_ref[...]

    qi = lax.broadcasted_iota(jnp.int32, (ATTN_BLOCK, 2 * ATTN_BLOCK), 0)
    kj = lax.broadcasted_iota(jnp.int32, (ATTN_BLOCK, 2 * ATTN_BLOCK), 1)
    dist = qi + ATTN_BLOCK - kj
    scale = HEAD_DIM ** -0.5

    for g, (refs, (window, dil)) in enumerate(zip(((q1, k1, v1), (q2, k2, v2), (q3, k3, v3)), ATTN_GROUPS)):
        q_ref, k_ref, v_ref = refs
        n_back = window // dil
        pad = ATTN_BLOCK * dil
        in_window = (dist >= 0) & (dist <= n_back)
        qs[...] = rotary(q_ref[...])
        ks[0:pad, :] = jnp.zeros((pad, LANES), F32)
        vs[0:pad, :] = jnp.zeros((pad, LANES), F32)
        ks[pad:pad + seq, :] = rotary(k_ref[...])
        vs[pad:pad + seq, :] = v_ref[...]

        def rows_of(idx, dil=dil, pad=pad):
            res = lax.rem(idx, dil)
            nb = idx // dil
            start = res + nb * pad
            if dil == 1:
                return nb, pl.ds(start, ATTN_BLOCK), pl.ds(start, 2 * ATTN_BLOCK)
            return nb, pl.ds(start, ATTN_BLOCK, stride=dil), pl.ds(start, 2 * ATTN_BLOCK, stride=dil)

        def blocks(i, carry, g=g, rows_of=rows_of, in_window=in_window):
            rows = [rows_of(i * ATTN_BPI + j) for j in range(ATTN_BPI)]
            ss = [_dot_nt(qs[rq, :], ks[rk, :]) * scale for _, rq, rk in rows]
            ps, ms = [], []
            for (nb, _, _), s in zip(rows, ss):
                s = jnp.where(in_window & ((nb > 0) | (kj >= ATTN_BLOCK)), s, -jnp.inf)
                m = jnp.max(s, axis=-1, keepdims=True)
                ms.append(m)
                ps.append(jnp.exp(s - m))
            accs = [_dot(p, vs[rk, :]) for (_, _, rk), p in zip(rows, ps)]
            for (_, rq, _), acc, m, p in zip(rows, accs, ms, ps):
                acc_s[g, rq, :] = acc
                m_s[g, rq, :] = jnp.broadcast_to(m, (ATTN_BLOCK, LANES))
                l_s[g, rq, :] = jnp.broadcast_to(jnp.sum(p, axis=-1, keepdims=True), (ATTN_BLOCK, LANES))
            return carry

        lax.fori_loop(0, nblk // ATTN_BPI, blocks, 0)

    def merge(i, carry):
        rows = pl.ds(pl.multiple_of(i * ATTN_BLOCK, ATTN_BLOCK), ATTN_BLOCK)
        ms = [m_s[g, rows, :] for g in range(3)]
        mx = jnp.maximum(jnp.maximum(ms[0], ms[1]), ms[2])
        ws = [jnp.exp(m - mx) for m in ms]
        num = ws[0] * acc_s[0, rows, :] + ws[1] * acc_s[1, rows, :] + ws[2] * acc_s[2, rows, :]
        den = ws[0] * l_s[0, rows, :] + ws[1] * l_s[1, rows, :] + ws[2] * l_s[2, rows, :]
        o_ref[rows, :] = (num / den).astype(o_ref.dtype)
        return carry

    lax.fori_loop(0, nblk, merge, 0)


def _attention(proj, cos_t, sin_t, bsz, seq):
    for window, dil in ATTN_GROUPS:
        assert window // dil == ATTN_BLOCK and seq % (ATTN_BLOCK * dil) == 0
    pad_max = ATTN_BLOCK * max(d for _, d in ATTN_GROUPS)

    def head_spec(col0, g):
        blk0 = col0 // HEAD_DIM + g * ATTN_HPG
        return pl.BlockSpec((seq, HEAD_DIM), lambda b, h: (b, blk0 + h))

    in_specs = []
    for g in range(len(ATTN_GROUPS)):
        in_specs += [head_spec(COL_AQ, g), head_spec(COL_AK, g), head_spec(COL_AV, g)]
    in_specs += [pl.BlockSpec((seq, LANES), lambda b, h: (0, 0))] * 2
    return pl.pallas_call(
        _attn_kernel,
        grid=(bsz, ATTN_HPG),
        in_specs=in_specs,
        out_specs=pl.BlockSpec((seq, HEAD_DIM), lambda b, h: (b, h)),
        out_shape=jax.ShapeDtypeStruct((bsz * seq, ATTN_HPG * HEAD_DIM), BF16),
        scratch_shapes=[pltpu.VMEM((seq, LANES), F32),
                        pltpu.VMEM((seq + pad_max, LANES), F32),
                        pltpu.VMEM((seq + pad_max, LANES), F32),
                        pltpu.VMEM((3, seq, LANES), F32),
                        pltpu.VMEM((3, seq, LANES), F32),
                        pltpu.VMEM((3, seq, LANES), F32)],
        compiler_params=_params("parallel", "parallel"),
        name="attention",
    )(*([proj] * 9), cos_t, sin_t)


def _bf16_parts(a, n):
    parts, rest = [], a
    for _ in range(n):
        piece = rest.astype(BF16).astype(F32)
        parts.append(piece)
        rest = rest - piece
    return parts


def _lhs3(a):
    hi, lo = _bf16_parts(a, 2)
    return jnp.concatenate([hi, hi, lo], axis=1).astype(BF16)


def _rhs3(b):
    hi, lo = _bf16_parts(b, 2)
    return jnp.concatenate([hi, lo, hi], axis=0).astype(BF16)


def _split3_rows(b):
    return jnp.concatenate(_bf16_parts(b, 3), axis=0).astype(BF16)


def _deltanet_kernel(q_ref, k_ref, v_ref, z_ref, ba_ref, cwq_ref, cwk_ref, cwv_ref, alog_ref, dtb_ref, nw_ref,
                     o_ref, kq_s, b_s, o_s, cd_s):
    seq = q_ref.shape[0]
    nchunk = seq // DN_CHUNK
    c64 = DN_CHUNK
    ri = lax.broadcasted_iota(jnp.int32, (c64, c64), 0)
    ci = lax.broadcasted_iota(jnp.int32, (c64, c64), 1)
    lower_incl = ri >= ci
    lower_strict = ri > ci
    ones_lower = lower_incl.astype(F32)
    eye = (ri == ci).astype(F32)
    lane = lax.broadcasted_iota(jnp.int32, (c64, LANES), 1)
    lane16 = lax.broadcasted_iota(jnp.int32, (1, DN_HEADS), 1)

    eye_bf16 = eye.astype(BF16)
    ones3 = jnp.concatenate([ones_lower] * 3, axis=1).astype(BF16)

    def conv_silu(x_ref, cw_ref, hh, r0, c):
        cols = slice(hh * DN_DIM, (hh + 1) * DN_DIM)
        prev = x_ref[pl.ds(pl.multiple_of(jnp.maximum(r0 - SUBLANES, 0), SUBLANES), SUBLANES), cols]
        prev = jnp.where(c > 0, prev, 0.0)
        win = jnp.concatenate([prev, x_ref[pl.ds(r0, c64), cols]], axis=0)
        y = win[SUBLANES:, :] * cw_ref[DN_CONV - 1:DN_CONV, cols]
        for j in range(1, DN_CONV):
            y = y + pltpu.roll(win, j, 0)[SUBLANES:, :] * cw_ref[DN_CONV - 1 - j:DN_CONV - j, cols]
        return _silu(y)

    def prepare(c, hh):
        r0 = pl.multiple_of(c * c64, c64)
        head = pl.program_id(1) * DN_HB + hh
        qx = conv_silu(q_ref, cwq_ref, hh, r0, c)
        kx = conv_silu(k_ref, cwk_ref, hh, r0, c)
        vx = conv_silu(v_ref, cwv_ref, hh, r0, c)
        qn = qx * lax.rsqrt(jnp.sum(qx * qx, axis=-1, keepdims=True) + 1e-6) * (DN_DIM ** -0.5)
        kn = kx * lax.rsqrt(jnp.sum(kx * kx, axis=-1, keepdims=True) + 1e-6)
        ba = ba_ref[pl.ds(r0, c64), :]
        b_col = jnp.sum(jnp.where(lane == head, ba, 0.0), axis=-1, keepdims=True)
        a_col = jnp.sum(jnp.where(lane == head + DN_HEADS, ba, 0.0), axis=-1, keepdims=True)
        a_log = jnp.sum(jnp.where(lane16 == head, alog_ref[...], 0.0), axis=-1, keepdims=True)
        dt_b = jnp.sum(jnp.where(lane16 == head, dtb_ref[...], 0.0), axis=-1, keepdims=True)
        beta = jax.nn.sigmoid(b_col)
        ax = a_col + dt_b
        softplus = jnp.maximum(ax, 0.0) + jnp.log1p(jnp.exp(-jnp.abs(ax)))
        g = -jnp.exp(a_log) * softplus
        gm = jnp.where(lower_strict, jnp.broadcast_to(g, (c64, c64)), 0.0)
        kb = kn * beta
        return dict(qn=qn, kn=kn, vb=vx * beta, kb=kb, g=g, gm3=_split3_rows(gm),
                    kbqn=jnp.concatenate([kb, qn], axis=0).astype(BF16), kn16=kn.astype(BF16))

    def local(i, carry):
        todo = [(i * DN_CPI + j, hh) for j in range(DN_CPI) for hh in range(DN_HB)]
        ps = [prepare(c, hh) for c, hh in todo]
        diffs = [_dot(ones3, p["gm3"]) for p in ps]
        scores = [_dot_nt(p["kbqn"], p["kn16"]) for p in ps]
        mids = []
        for p, diff, sc in zip(ps, diffs, scores):
            gam = diff[:, 0:1] + p["g"][0:1, :]
            gam_last = gam[c64 - 1:c64, :]
            decay = jnp.where(lower_incl, jnp.exp(jnp.where(lower_incl, diff, 0.0)), 0.0)
            eg = jnp.exp(gam)
            mids.append(dict(nmat=-jnp.where(lower_strict, sc[:c64] * decay, 0.0), qk=sc[c64:] * decay,
                             rhs=jnp.concatenate([p["vb"], p["kb"] * eg], axis=-1), qd=p["qn"] * eg,
                             kd=(p["kn"] * jnp.exp(gam_last - gam)).astype(BF16),
                             cd=jnp.broadcast_to(jnp.exp(gam_last), (SUBLANES, LANES))))
        invs = [eye + m["nmat"] for m in mids]
        pws = [_dot(_lhs3(m["nmat"]), _rhs3(m["nmat"])) for m in mids]
        for _ in range(4):
            boths = [_dot(_lhs3(pw), _rhs3(jnp.concatenate([inv, pw], axis=1))) for inv, pw in zip(invs, pws)]
            invs = [inv + b[:, :c64] for inv, b in zip(invs, boths)]
            pws = [b[:, c64:] for b in boths]
        ys = [_dot(_lhs3(inv), _rhs3(m["rhs"])) for inv, m in zip(invs, mids)]
        sols = [y + _dot(_lhs3(pw), _rhs3(y)) for y, pw in zip(ys, pws)]
        kdts = [_dot_tn(m["kd"], eye_bf16) for m in mids]
        prods = [_dot(jnp.concatenate([kd_t, m["qk"]], axis=0).astype(BF16),
                      jnp.concatenate([sol[:, DN_DIM:], sol[:, :DN_DIM]], axis=1).astype(BF16))
                 for m, sol, kd_t in zip(mids, sols, kdts)]
        for (c, hh), m, pr in zip(todo, mids, prods):
            kq_s[hh, pl.ds(pl.multiple_of(c * (DN_DIM + c64), c64), DN_DIM + c64), :] = jnp.concatenate(
                [pr[:DN_DIM, :DN_DIM], m["qd"] - pr[DN_DIM:, :DN_DIM]], axis=0).astype(BF16)
            b_s[hh, pl.ds(pl.multiple_of(c * DN_DIM, DN_DIM), DN_DIM), :] = pr[:DN_DIM, DN_DIM:].astype(BF16)
            o_s[hh, pl.ds(pl.multiple_of(c * c64, c64), c64), :] = pr[DN_DIM:, DN_DIM:].astype(BF16)
            cd_s[hh, pl.ds(pl.multiple_of(c * SUBLANES, SUBLANES), SUBLANES), :] = m["cd"]
        return carry

    lax.fori_loop(0, nchunk // DN_CPI, local, 0)

    def scan(c, states):
        r0 = pl.multiple_of(c * c64, c64)
        rows = pl.ds(r0, c64)
        heads = range(DN_HB)
        rkq = pl.ds(pl.multiple_of(c * (DN_DIM + c64), c64), DN_DIM + c64)
        rb = pl.ds(pl.multiple_of(c * DN_DIM, DN_DIM), DN_DIM)
        cds = [cd_s[hh, pl.ds(pl.multiple_of(c * SUBLANES, SUBLANES), SUBLANES), :][0:1, :] for hh in heads]
        rs = [_dot(kq_s[hh, rkq, :], states[hh].astype(BF16)) for hh in heads]
        new_states = [states[hh] * cds[hh] - rs[hh][:DN_DIM] + b_s[hh, rb, :] for hh in heads]
        outs = []
        for hh in heads:
            o = rs[hh][DN_DIM:] + o_s[hh, rows, :]
            on = o * lax.rsqrt(jnp.mean(o * o, axis=-1, keepdims=True) + NORM_EPS) * nw_ref[...]
            outs.append((on * _silu(z_ref[rows, hh * DN_DIM:(hh + 1) * DN_DIM])).astype(o_ref.dtype))
        for hh in heads:
            o_ref[rows, hh * DN_DIM:(hh + 1) * DN_DIM] = outs[hh]
        return tuple(new_states)

    lax.fori_loop(0, nchunk, scan, tuple(jnp.zeros((DN_DIM, DN_DIM), F32) for _ in range(DN_HB)))


def _deltanet(proj, conv_w, a_log, dt_bias, norm_w, bsz, seq):
    wb = DN_DIM * DN_HB
    hblocks = DN_HEADS // DN_HB
    nchunk = seq // DN_CHUNK
    assert nchunk % DN_CPI == 0

    def act_spec(col0):
        blk0 = col0 // wb
        return pl.BlockSpec((seq, wb), lambda b, h: (b, blk0 + h))

    def cw_spec(part):
        blk0 = part * DN_WIDTH // wb
        return pl.BlockSpec((DN_CONV, wb), lambda b, h: (0, blk0 + h))

    return pl.pallas_call(
        _deltanet_kernel,
        grid=(bsz, hblocks),
        in_specs=[act_spec(COL_DQ), act_spec(COL_DK), act_spec(COL_DV), act_spec(COL_DZ),
                  pl.BlockSpec((seq, LANES), lambda b, h: (b, COL_BA // LANES)),
                  cw_spec(0), cw_spec(1), cw_spec(2),
                  pl.BlockSpec((1, DN_HEADS), lambda b, h: (0, 0)),
                  pl.BlockSpec((1, DN_HEADS), lambda b, h: (0, 0)),
                  pl.BlockSpec((1, DN_DIM), lambda b, h: (0, 0))],
        out_specs=pl.BlockSpec((seq, wb), lambda b, h: (b, h)),
        out_shape=jax.ShapeDtypeStruct((bsz * seq, DN_WIDTH), BF16),
        scratch_shapes=[pltpu.VMEM((DN_HB, nchunk * (DN_DIM + DN_CHUNK), DN_DIM), BF16),
                        pltpu.VMEM((DN_HB, nchunk * DN_DIM, DN_DIM), BF16),
                        pltpu.VMEM((DN_HB, seq, DN_DIM), BF16),
                        pltpu.VMEM((DN_HB, nchunk * SUBLANES, LANES), F32)],
        compiler_params=_params("parallel", "parallel"),
        name="deltanet",
    )(proj, proj, proj, proj, proj, conv_w, conv_w, conv_w,
      a_log.reshape(1, DN_HEADS), dt_bias.reshape(1, DN_HEADS), norm_w.reshape(1, DN_DIM))


def _mixout_kernel(ya_ref, og_ref, ga_ref, gd_ref, x_ref, ada_ref, wa_ref, wd_ref, wm_ref, npost_ref, npre_ref,
                   xmid_ref, h2_ref):
    y_attn = _dot(ya_ref[...], wa_ref[...])
    y_delta = _dot(og_ref[...], wd_ref[...])
    merged = jax.nn.sigmoid(ga_ref[...]) * y_attn + jax.nn.sigmoid(gd_ref[...]) * y_delta
    y = _dot(merged.astype(BF16), wm_ref[...])
    yn = y * lax.rsqrt(jnp.mean(y * y, axis=-1, keepdims=True) + NORM_EPS) * npost_ref[...]
    x_mid = x_ref[...] + ada_ref[2:3, :] * yn
    xmid_ref[...] = x_mid
    hn = x_mid * lax.rsqrt(jnp.mean(x_mid * x_mid, axis=-1, keepdims=True) + NORM_EPS) * npre_ref[...]
    h2_ref[...] = hn * (1.0 + ada_ref[4:5, :]) + ada_ref[3:4, :]


def _mixout(y_attn, o_gated, proj, x2, ada3, w_attn, w_delta, w_mix, norm_post, norm_pre_ffn, seq):
    t, d = x2.shape
    tm = 256
    const = dict(pipeline_mode=pl.Buffered(1))
    return pl.pallas_call(
        _mixout_kernel,
        grid=(t // tm,),
        in_specs=[pl.BlockSpec((tm, y_attn.shape[1]), lambda i: (i, 0)),
                  pl.BlockSpec((tm, d), lambda i: (i, 0)),
                  pl.BlockSpec((tm, d), lambda i: (i, COL_GA // d)),
                  pl.BlockSpec((tm, d), lambda i: (i, COL_GD // d)),
                  pl.BlockSpec((tm, d), lambda i: (i, 0)),
                  pl.BlockSpec((None, 6, d), lambda i: (i * tm // seq, 0, 0)),
                  pl.BlockSpec(w_attn.shape, lambda i: (0, 0), **const),
                  pl.BlockSpec(w_delta.shape, lambda i: (0, 0), **const),
                  pl.BlockSpec(w_mix.shape, lambda i: (0, 0), **const),
                  pl.BlockSpec((1, d), lambda i: (0, 0)),
                  pl.BlockSpec((1, d), lambda i: (0, 0))],
        out_specs=[pl.BlockSpec((tm, d), lambda i: (i, 0))] * 2,
        out_shape=[jax.ShapeDtypeStruct((t, d), F32)] * 2,
        compiler_params=_params("parallel"),
        name="mix_out",
    )(y_attn, o_gated, proj, proj, x2, ada3, w_attn, w_delta, w_mix, norm_post, norm_pre_ffn)


def _topk_rows(s, k):
    n = s.shape[0]
    rid = lax.broadcasted_iota(jnp.int32, s.shape, 0)
    vals, ids = [], []
    for _ in range(k):
        m = jnp.max(s, axis=0, keepdims=True)
        sel = jnp.min(jnp.where(s == m, rid, n), axis=0, keepdims=True)
        vals.append(m)
        ids.append(sel)
        s = jnp.where(rid == sel, -jnp.inf, s)
    return jnp.concatenate(vals, axis=0), jnp.concatenate(ids, axis=0)


def _peer_route_kernel(h_ref, wq_ref, keys_ref, idx_ref, gw_ref, q_s, idx_s, gw_s):
    tm = h_ref.shape[0]
    hblock = pl.program_id(1)

    @pl.when(hblock == 0)
    def _():
        q = _dot(h_ref[...].astype(BF16), wq_ref[...])
        for hh in range(PEER_HEADS):
            q_s[hh] = q[:, hh * PEER_QDIM:(hh + 1) * PEER_QDIM]

    for hh in range(PEER_HPS):
        _route_head(hblock * PEER_HPS + hh, q_s, keys_ref.at[hh], idx_s, gw_s, tm)

    @pl.when(hblock == PEER_HEADS // PEER_HPS - 1)
    def _():
        for bb in range(tm // PEER_TB):
            tok = slice(bb * PEER_TB, (bb + 1) * PEER_TB)
            gw_ref[bb] = gw_s[:, tok]
            idx_ref[bb] = idx_s[:, tok].T


def _route_head(head, q_s, keys_ref, idx_s, gw_s, tm):
    half = PEER_QDIM // 2
    qh = q_s[head]
    tops = []
    for p in range(2):
        keys_hi, keys_lo = _bf16_parts(keys_ref[p], 2)
        q_hi, q_lo = _bf16_parts(qh[:, p * half:(p + 1) * half], 2)
        s = _dot_nt(jnp.concatenate([keys_hi, keys_hi, keys_lo], axis=1).astype(BF16),
                    jnp.concatenate([q_hi, q_lo, q_hi], axis=1).astype(BF16))
        tops.append(_topk_rows(s, PEER_TOPK))
    (s0, i0), (s1, i1) = tops
    counts = [PEER_TOPK // (a + 1) for a in range(PEER_TOPK)]
    n_cand = sum(counts)
    n_pad = -n_cand % SUBLANES
    cand = jnp.concatenate([s0[a:a + 1, :] + s1[0:counts[a], :] for a in range(PEER_TOPK)]
                           + [jnp.full((n_pad, tm), -jnp.inf, F32)], axis=0)
    best_s, best_row = _topk_rows(cand, PEER_TOPK)
    a_sel = jnp.zeros_like(best_row)
    b_sel = best_row
    start = 0
    for a in range(PEER_TOPK - 1):
        start += counts[a]
        past = (best_row >= start).astype(jnp.int32)
        a_sel = a_sel + past
        b_sel = b_sel - past * counts[a]
    e0 = jnp.zeros_like(best_row)
    e1 = jnp.zeros_like(best_row)
    for a in range(PEER_TOPK):
        e0 = e0 + jnp.where(a_sel == a, i0[a:a + 1, :], 0)
        e1 = e1 + jnp.where(b_sel == a, i1[a:a + 1, :], 0)
    ex = jnp.exp(best_s - jnp.max(best_s, axis=0, keepdims=True))
    rows = pl.ds(pl.multiple_of(head * PEER_TOPK, PEER_TOPK), PEER_TOPK)
    idx_s[rows, :] = e0 * PEER_KEYS + e1
    gw_s[rows, :] = ex / jnp.sum(ex, axis=0, keepdims=True)


def _peer_route(h2, w_query_bf16, keys):
    t, d = h2.shape
    tm = 256
    nb = tm // PEER_TB
    return pl.pallas_call(
        _peer_route_kernel,
        grid=(t // tm, PEER_HEADS // PEER_HPS),
        in_specs=[pl.BlockSpec((tm, d), lambda i, h: (i, 0)),
                  pl.BlockSpec(w_query_bf16.shape, lambda i, h: (0, 0)),
                  pl.BlockSpec((PEER_HPS, 2, PEER_KEYS, PEER_QDIM // 2), lambda i, h: (h, 0, 0, 0))],
        out_specs=[pl.BlockSpec((nb, PEER_TB, PEER_SLOTS), lambda i, h: (i, 0, 0)),
                   pl.BlockSpec((nb, PEER_SLOTS, PEER_TB), lambda i, h: (i, 0, 0))],
        out_shape=[jax.ShapeDtypeStruct((t // PEER_TB, PEER_TB, PEER_SLOTS), jnp.int32),
                   jax.ShapeDtypeStruct((t // PEER_TB, PEER_SLOTS, PEER_TB), F32)],
        scratch_shapes=[pltpu.VMEM((PEER_HEADS, tm, PEER_QDIM), F32),
                        pltpu.VMEM((PEER_SLOTS, tm), jnp.int32),
                        pltpu.VMEM((PEER_SLOTS, tm), F32)],
        compiler_params=_params("parallel", "arbitrary"),
        name="peer_route",
    )(h2, w_query_bf16, keys)


def _peer_gather_kernel(idx_hbm, gw_ref, h_ref, xmid_ref, ada_ref, nw_ref, tbl_hbm,
                        out_ref, buf, sem, idx_sm, idx_sem, ybuf):
    tb, d = h_ref.shape
    half = d // 2
    nsub = half // LANES
    slot_rows = PEER_SLOTS * PEER_PITCH
    nbuf = PEER_NBUF
    ngroups = tb // nbuf
    step = pl.program_id(0)
    lane = lax.broadcasted_iota(jnp.int32, (PEER_SLOTS, tb), 1)
    hi_mask = jnp.uint32(0xFFFF0000)

    def fetch_idx(g, par):
        return pltpu.make_async_copy(idx_hbm.at[step, pl.ds(g * nbuf, nbuf)], idx_sm.at[par], idx_sem.at[par])

    def issue(par, j):
        for k in range(PEER_SLOTS):
            dst = buf.at[pl.ds(j * slot_rows + k * PEER_PITCH, 2 * nsub)]
            pltpu.make_async_copy(tbl_hbm.at[idx_sm[par, j, k]], dst, sem.at[j]).start(priority=k % 2)

    def wait(j):
        rows = pl.ds(0, PEER_SLOTS * 2 * nsub)
        pltpu.make_async_copy(buf.at[rows], buf.at[rows], sem.at[j]).wait()

    def tile(j, tab, s):
        w = buf[pl.ds(j * slot_rows + tab * nsub + s, PEER_SLOTS, stride=PEER_PITCH), :]
        return pltpu.bitcast(w << 16, F32), pltpu.bitcast(w & hi_mask, F32)

    def compute(t, j):
        xrow = h_ref[pl.ds(t, 1), :]
        acc = None
        for s in range(nsub):
            lo, hi = tile(j, 0, s)
            term = lo * xrow[:, s * LANES:(s + 1) * LANES] + hi * xrow[:, half + s * LANES:half + (s + 1) * LANES]
            acc = term if acc is None else acc + term
        hcol = jnp.sum(acc, axis=-1, keepdims=True)
        gwcol = jnp.sum(jnp.where(lane == t, gw_ref[0], 0.0), axis=-1, keepdims=True)
        ccol = gwcol * _gelu_exact(hcol)
        los, his = [], []
        for s in range(nsub):
            lo, hi = tile(j, 1, s)
            los.append(jnp.sum(lo * ccol, axis=0, keepdims=True))
            his.append(jnp.sum(hi * ccol, axis=0, keepdims=True))
        ybuf[pl.ds(t, 1), :] = jnp.concatenate(los + his, axis=-1)

    def run_group(g, par_next, fetch_g):
        fetch_idx(g + 1, par_next).wait()
        if fetch_g is not None:
            fetch_idx(fetch_g, 1 - par_next).start()
        for j in range(nbuf):
            wait(j)
            compute(g * nbuf + j, j)
            issue(par_next, j)

    fetch_idx(0, 0).start()
    fetch_idx(0, 0).wait()
    fetch_idx(1, 1).start()
    for j in range(nbuf):
        issue(0, j)

    def pair(p, carry):
        run_group(2 * p, 1, 2 * p + 2)
        run_group(2 * p + 1, 0, 2 * p + 3)
        return carry

    lax.fori_loop(0, ngroups // 2 - 1, pair, 0)
    run_group(ngroups - 2, 1, None)
    for j in range(nbuf):
        wait(j)
        compute(tb - nbuf + j, j)
    y = ybuf[...]
    yn = y * lax.rsqrt(jnp.mean(y * y, axis=-1, keepdims=True) + NORM_EPS) * nw_ref[...]
    out_ref[...] = xmid_ref[...] + ada_ref[5:6, :] * yn


def _pack_tables_kernel(down_ref, up_ref, o_ref):
    half = down_ref.shape[1] // 2
    hi_mask = jnp.uint32(0xFFFF0000)

    def pack(x):
        lo = pltpu.bitcast(x[:, :half].astype(BF16).astype(F32), jnp.uint32)
        hi = pltpu.bitcast(x[:, half:].astype(BF16).astype(F32), jnp.uint32)
        return (lo >> 16) | (hi & hi_mask)

    o_ref[:, :half] = pack(down_ref[...])
    o_ref[:, half:] = pack(up_ref[...])


def _pack_expert_tables(down, up):
    rows, d = down.shape
    tr = 256
    packed = pl.pallas_call(
        _pack_tables_kernel,
        grid=(rows // tr,),
        in_specs=[pl.BlockSpec((tr, d), lambda i: (i, 0))] * 2,
        out_specs=pl.BlockSpec((tr, d), lambda i: (i, 0)),
        out_shape=jax.ShapeDtypeStruct((rows, d), jnp.uint32),
        compiler_params=_params("parallel"),
        name="pack_tables",
    )(down, up)
    return packed.reshape(rows, d // LANES, LANES)


def _peer_gather(idx_t, gw_t, h2, x_mid, ada3, norm_w, tables, seq):
    t, d = h2.shape
    blocks_per_batch = seq // PEER_TB
    nsub = d // 2 // LANES
    assert PEER_TB % (2 * PEER_NBUF) == 0 and 2 * nsub < PEER_PITCH
    return pl.pallas_call(
        _peer_gather_kernel,
        grid=(t // PEER_TB,),
        in_specs=[pl.BlockSpec(memory_space=pl.ANY),
                  pl.BlockSpec((1, PEER_SLOTS, PEER_TB), lambda i: (i, 0, 0)),
                  pl.BlockSpec((PEER_TB, d), lambda i: (i, 0)),
                  pl.BlockSpec((PEER_TB, d), lambda i: (i, 0)),
                  pl.BlockSpec((None, 6, d), lambda i: (i // blocks_per_batch, 0, 0)),
                  pl.BlockSpec((1, d), lambda i: (0, 0)),
                  pl.BlockSpec(memory_space=pl.ANY)],
        out_specs=pl.BlockSpec((PEER_TB, d), lambda i: (i, 0)),
        out_shape=jax.ShapeDtypeStruct((t, d), F32),
        scratch_shapes=[pltpu.VMEM((PEER_NBUF * PEER_SLOTS * PEER_PITCH, LANES), jnp.uint32),
                        pltpu.SemaphoreType.DMA((PEER_NBUF,)),
                        pltpu.SMEM((2, PEER_NBUF, PEER_SLOTS), jnp.int32),
                        pltpu.SemaphoreType.DMA((2,)),
                        pltpu.VMEM((PEER_TB, d), F32)],
        compiler_params=_params("arbitrary"),
        name="peer_gather",
    )(idx_t, gw_t, h2, x_mid, ada3, norm_w, tables)


def _permute_w_in(w_in):
    o_attn, o_dn, o_z = 0, 3 * ATTN_WIDTH, 3 * ATTN_WIDTH + 3 * DN_WIDTH
    o_b = o_z + DN_WIDTH
    o_gates = o_b + 2 * DN_HEADS
    parts = [w_in[:, o_gates:o_gates + 4096], w_in[:, o_dn:o_dn + 3 * DN_WIDTH], w_in[:, o_z:o_z + DN_WIDTH],
             w_in[:, o_attn:o_attn + 3 * ATTN_WIDTH], w_in[:, o_b:o_b + 2 * DN_HEADS]]
    w = jnp.concatenate(parts, axis=1)
    return jnp.pad(w, ((0, 0), (0, PROJ_WIDTH - w.shape[1]))).astype(BF16)


def _layer(x2, c, bsz, seq, w_ada, b_ada, norm_pre_mix, norm_post_mix, norm_pre_ffn, norm_post_ffn, w_in, conv_w,
           a_log, dt_bias, dn_norm_w, w_attn_out, w_delta_out, w_mix_out, peer_w_query, peer_sub_keys, peer_down,
           peer_up, cos_t, sin_t):
    d = x2.shape[1]
    row = lambda v: v.reshape(1, -1)
    ada3 = _ada(c, w_ada, b_ada).reshape(bsz, 6, d)
    proj = _inproj(x2, ada3, row(norm_pre_mix), _permute_w_in(w_in), seq)
    y_attn = _attention(proj, cos_t, sin_t, bsz, seq)
    o_gated = _deltanet(proj, conv_w, a_log, dt_bias, dn_norm_w, bsz, seq)
    x_mid, h2 = _mixout(y_attn, o_gated, proj, x2, ada3, w_attn_out.astype(BF16), w_delta_out.astype(BF16),
                        w_mix_out.astype(BF16), row(norm_post_mix), row(norm_pre_ffn), seq)
    idx_t, gw_t = _peer_route(h2, peer_w_query.astype(BF16), peer_sub_keys)
    return _peer_gather(idx_t, gw_t, h2, x_mid, ada3, row(norm_post_ffn), _pack_expert_tables(peer_down, peer_up),
                        seq)


def kernel(x, c, w_ada, b_ada, norm_pre_mix, norm_post_mix, norm_pre_ffn, norm_post_ffn, w_in, conv_w, a_log, dt_bias, dn_norm_w, w_attn_out, w_delta_out, w_mix_out, peer_w_query, peer_sub_keys, peer_down, peer_up):
    bsz, seq, d = x.shape
    x2 = x.reshape(bsz * seq, d)
    cos_t, sin_t = _rope_tables(seq)
    for layer in range(w_ada.shape[0]):
        x2 = _layer(x2, c, bsz, seq, w_ada[layer], b_ada[layer], norm_pre_mix[layer], norm_post_mix[layer],
                    norm_pre_ffn[layer], norm_post_ffn[layer], w_in[layer], conv_w[layer], a_log[layer],
                    dt_bias[layer], dn_norm_w[layer], w_attn_out[layer], w_delta_out[layer], w_mix_out[layer],
                    peer_w_query[layer], peer_sub_keys[layer], peer_down[layer], peer_up[layer], cos_t, sin_t)
    return x2.reshape(bsz, seq, d)
```
